```python
import jax, jax.numpy as jnp
from jax import lax
import numpy as np

D_MODEL = 1024
BATCH = 32
SEQ = 256
DEPTH = 4
DEC_BATCH = 8
DEC_SEQ = 2048
PAST_LEN = 256

GRID_W = 64
HEAD_DIM = 64
N_HEADS = D_MODEL // HEAD_DIM
H_RET = N_HEADS // 2
H_NA = N_HEADS - H_RET
RET_DK = HEAD_DIM
RET_DV = HEAD_DIM
RET_CHUNK = 128
WIN_R = 8
WIN_C = 16
NA_QB = 16
NA_KBW = NA_QB + WIN_C
NA_NCB = GRID_W // NA_QB
MLA_HEADS = N_HEADS
MLA_Q_LORA = 384
MLA_KV_LORA = 256
MLA_NOPE = 64
MLA_ROPE = 32
MLA_V = 64
MLA_SCALE = (MLA_NOPE + MLA_ROPE) ** -0.5
ROPE_BASE = 10000.0
D_FF = 4 * D_MODEL
Q_BLOCK = 128
N_A_LAYERS = (DEPTH + 1) // 2
N_C_LAYERS = DEPTH // 2
EPS = 1e-6
NEG_INF = -1e30
AC_IN = 2 * H_RET * RET_DK + 2 * H_RET * RET_DV + 3 * H_NA * HEAD_DIM
AC_OUT = H_RET * RET_DV + H_NA * HEAD_DIM
C_IN = MLA_Q_LORA + MLA_KV_LORA + MLA_ROPE

kernel_name = 'hybrid_retention_na_mla_flow_step'


def rms_norm(x, g):
    xf = x.astype(jnp.float32)
    y = xf * lax.rsqrt(jnp.mean(xf * xf, axis=-1, keepdims=True) + EPS)
    return (y * g.astype(jnp.float32)).astype(x.dtype)


def split_heads(x, h):
    b, l, _ = x.shape
    return x.reshape(b, l, h, -1).transpose(0, 2, 1, 3)


def merge_heads(x):
    b, h, l, d = x.shape
    return x.transpose(0, 2, 1, 3).reshape(b, l, h * d)


def axial_rope(x):
    l = x.shape[-2]
    t = jnp.arange(l)
    row = (t // GRID_W).astype(jnp.float32)
    col = (t % GRID_W).astype(jnp.float32)
    nf = MLA_ROPE // 4
    inv = ROPE_BASE ** (-jnp.arange(nf, dtype=jnp.float32) / nf)
    ang = jnp.concatenate([row[:, None] * inv, col[:, None] * inv], axis=-1)
    cos, sin = jnp.cos(ang), jnp.sin(ang)
    xf = x.astype(jnp.float32).reshape(x.shape[:-1] + (MLA_ROPE // 2, 2))
    x1, x2 = xf[..., 0], xf[..., 1]
    out = jnp.stack([x1 * cos - x2 * sin, x1 * sin + x2 * cos], axis=-1)
    return out.reshape(x.shape).astype(x.dtype)


def joint_attention(qs, ks, vs, scale):
    b, h, l, _ = qs[0].shape
    nb = l // Q_BLOCK
    splits = [int(s) for s in np.cumsum([k.shape[2] for k in ks])[:-1]]

    def block(i):
        start = i * Q_BLOCK
        s = jnp.concatenate(
            [jnp.einsum('bhqd,bhkd->bhqk', lax.dynamic_slice_in_dim(q, start, Q_BLOCK, axis=2), k)
             for q, k in zip(qs, ks)], axis=-1).astype(jnp.float32) * scale
        p = jax.nn.softmax(s, axis=-1)
        ps = jnp.split(p, splits, axis=-1)
        out = jnp.einsum('bhqk,bhkd->bhqd', ps[0].astype(vs[0].dtype), vs[0])
        for j in range(1, len(vs)):
            out = out + jnp.einsum('bhqk,bhkd->bhqd', ps[j].astype(vs[j].dtype), vs[j])
        return out

    outs = lax.map(block, jnp.arange(nb))
    return jnp.moveaxis(outs, 0, 2).reshape(b, h, l, -1)


def log_decay(p):
    return jnp.log1p(-jnp.exp2(p.astype(jnp.float32)))


def retention_chunkwise(q, k, v, lg, s0):
    dtype = v.dtype
    q, k, v = q.astype(jnp.float32), k.astype(jnp.float32), v.astype(jnp.float32)
    b, h, l, dk = q.shape
    dv = v.shape[-1]
    n = l // RET_CHUNK
    qc = q.reshape(b, h, n, RET_CHUNK, dk)
    kc = k.reshape(b, h, n, RET_CHUNK, dk)
    vc = v.reshape(b, h, n, RET_CHUNK, dv)
    pos = jnp.arange(RET_CHUNK, dtype=jnp.float32)
    lgc = lg[:, None]
    diff = pos[:, None] - pos[None, :]
    dmask = jnp.where(diff >= 0, jnp.exp(lgc[:, :, None] * jnp.maximum(diff, 0.0)), 0.0)
    s = jnp.einsum('bhncd,bhnmd->bhncm', qc, kc) * dmask[None, :, None]
    o_intra = jnp.einsum('bhncm,bhnme->bhnce', s, vc)
    k_dec = jnp.exp(lgc * (RET_CHUNK - 1 - pos))
    kv = jnp.einsum('bhnmd,bhnme->bhnde', kc * k_dec[None, :, None, :, None], vc)
    chunk_dec = jnp.exp(lg * RET_CHUNK)[:, None, None]

    def step(state, kv_i):
        return chunk_dec * state + kv_i, state

    s_fin, s_prev = lax.scan(step, s0.astype(jnp.float32), jnp.moveaxis(kv, 2, 0))
    q_dec = jnp.exp(lgc * (pos + 1.0))
    o_cross = jnp.einsum('bhncd,nbhde->bhnce', qc * q_dec[None, :, None, :, None], s_prev)
    return (o_intra + o_cross).reshape(b, h, l, dv).astype(dtype), s_fin.astype(dtype)


def retention_bidir(q, k, v, lg_f, lg_b, s_f, s_b):
    o_f, fin_f = retention_chunkwise(q, k, v, lg_f, s_f)
    o_b, fin_b = retention_chunkwise(jnp.flip(q, 2), jnp.flip(k, 2), jnp.flip(v, 2), lg_b, s_b)
    return o_f + jnp.flip(o_b, 2), fin_f, fin_b


def retention_out(o, g):
    of = o.astype(jnp.float32)
    mu = jnp.mean(of, axis=-1, keepdims=True)
    var = jnp.mean(jnp.square(of - mu), axis=-1, keepdims=True)
    on = ((of - mu) * lax.rsqrt(var + EPS)).astype(o.dtype)
    return merge_heads(on) * jax.nn.silu(g)


def even_project(h, w_in):
    sizes = [H_RET * RET_DK, H_RET * RET_DK, H_RET * RET_DV, H_RET * RET_DV,
             H_NA * HEAD_DIM, H_NA * HEAD_DIM, H_NA * HEAD_DIM]
    parts = jnp.split(h @ w_in, [int(s) for s in np.cumsum(sizes)[:-1]], axis=-1)
    rq, rk, rv, rg, nq, nk, nv = parts
    rq = split_heads(rq, H_RET)
    rk = split_heads(rk, H_RET) * (RET_DK ** -0.5)
    rv = split_heads(rv, H_RET)
    return rq, rk, rv, rg, split_heads(nq, H_NA), split_heads(nk, H_NA), split_heads(nv, H_NA)


def neighborhood_attention(q, k, v, k_ctx, v_ctx, rpb):
    b, h, l, d = q.shape
    rows = l // GRID_W
    wr = min(WIN_R, rows)
    r = np.arange(rows)
    row_idx = np.clip(r - wr // 2, 0, rows - wr)[:, None] + np.arange(wr)[None]
    blk = np.arange(NA_NCB)
    col_start = np.clip(blk * NA_QB - WIN_C // 2, 0, GRID_W - NA_KBW)
    col_idx = col_start[:, None] + np.arange(NA_KBW)[None]
    q_col = blk[:, None] * NA_QB + np.arange(NA_QB)[None]
    win_start = np.clip(q_col - WIN_C // 2, 0, GRID_W - WIN_C)
    valid = (col_idx[:, None, :] >= win_start[:, :, None]) & (col_idx[:, None, :] < win_start[:, :, None] + WIN_C)
    dr = row_idx - r[:, None] + (WIN_R - 1)
    dc = np.clip(col_idx[:, None, :] - q_col[:, :, None], -(WIN_C - 1), WIN_C - 1) + (WIN_C - 1)
    bias = rpb[:, dr[:, None, None, :, None], dc[None, :, :, None, :]].astype(jnp.float32)
    bias = jnp.where(valid[None, None, :, :, None, :], bias, NEG_INF).reshape(h, rows, NA_NCB, NA_QB, wr * NA_KBW)
    qg = q.reshape(b, h, rows, NA_NCB, NA_QB, d)
    gidx = (row_idx[:, None, :, None], col_idx[None, :, None, :])
    kg = k.reshape(b, h, rows, GRID_W, d)[:, :, gidx[0], gidx[1]].reshape(b, h, rows, NA_NCB, wr * NA_KBW, d)
    vg = v.reshape(b, h, rows, GRID_W, d)[:, :, gidx[0], gidx[1]].reshape(b, h, rows, NA_NCB, wr * NA_KBW, d)
    scale = HEAD_DIM ** -0.5
    s_loc = jnp.einsum('bhrnqd,bhrnkd->bhrnqk', qg, kg).astype(jnp.float32) * scale + bias[None]
    s_ctx = jnp.einsum('bhrnqd,bhkd->bhrnqk', qg, k_ctx).astype(jnp.float32) * scale
    p = jax.nn.softmax(jnp.concatenate([s_loc, s_ctx], axis=-1), axis=-1).astype(v.dtype)
    nloc = wr * NA_KBW
    o = (jnp.einsum('bhrnqk,bhrnkd->bhrnqd', p[..., :nloc], vg)
         + jnp.einsum('bhrnqk,bhkd->bhrnqd', p[..., nloc:], v_ctx))
    return o.reshape(b, h, l, d)


def mixer_ac_context(h, w_in, w_out, lg_f, lg_b):
    rq, rk, rv, rg, nq, nk, nv = even_project(h, w_in)
    zeros = jnp.zeros((h.shape[0], H_RET, RET_DK, RET_DV), h.dtype)
    o_ret, s_f, s_b = retention_bidir(rq, rk, rv, lg_f, lg_b, zeros, zeros)
    o_na = joint_attention([nq], [nk], [nv], HEAD_DIM ** -0.5)
    y = jnp.concatenate([retention_out(o_ret, rg), merge_heads(o_na)], axis=-1) @ w_out
    return y, s_f, s_b, nk, nv


def mixer_ac_latent(h, w_in, w_out, lg_f, lg_b, s_f, s_b, k_ctx, v_ctx, rpb):
    rq, rk, rv, rg, nq, nk, nv = even_project(h, w_in)
    o_ret, _, _ = retention_bidir(rq, rk, rv, lg_f, lg_b, s_f, s_b)
    o_na = neighborhood_attention(nq, nk, nv, k_ctx, v_ctx, rpb)
    return jnp.concatenate([retention_out(o_ret, rg), merge_heads(o_na)], axis=-1) @ w_out


def mla_project(h, w_in, q_norm, kv_norm, w_uq):
    c_q, c_kv, k_pe = jnp.split(h @ w_in, [MLA_Q_LORA, MLA_Q_LORA + MLA_KV_LORA], axis=-1)
    c_kv = rms_norm(c_kv, kv_norm)
    q = split_heads(rms_norm(c_q, q_norm) @ w_uq, MLA_HEADS)
    return q[..., :MLA_NOPE], q[..., MLA_NOPE:], c_kv, k_pe


def bcast_heads(k_pe):
    b, l, r = k_pe.shape
    return jnp.broadcast_to(k_pe[:, None], (b, MLA_HEADS, l, r))


def mla_context(h, w_in, q_norm, kv_norm, w_uq, w_uk, w_uv, w_out):
    q_nope, q_pe, c_kv, k_pe = mla_project(h, w_in, q_norm, kv_norm, w_uq)
    k_nope = split_heads(c_kv @ w_uk, MLA_HEADS)
    v = split_heads(c_kv @ w_uv, MLA_HEADS)
    q = jnp.concatenate([q_nope, q_pe], axis=-1)
    k = jnp.concatenate([k_nope, bcast_heads(k_pe)], axis=-1)
    o = joint_attention([q], [k], [v], MLA_SCALE)
    return merge_heads(o) @ w_out, c_kv, k_pe


def mla_latent(h, w_in, q_norm, kv_norm, w_uq, w_uk, w_uv, w_out, ckv_ctx, kpe_ctx):
    q_nope, q_pe, c_kv, k_pe = mla_project(h, w_in, q_norm, kv_norm, w_uq)
    k_nope = split_heads(c_kv @ w_uk, MLA_HEADS)
    v = split_heads(c_kv @ w_uv, MLA_HEADS)
    kc_nope = split_heads(ckv_ctx @ w_uk, MLA_HEADS)
    vc = split_heads(ckv_ctx @ w_uv, MLA_HEADS)
    q_lat = jnp.concatenate([q_nope, axial_rope(q_pe)], axis=-1)
    k_lat = jnp.concatenate([k_nope, bcast_heads(axial_rope(k_pe))], axis=-1)
    q_ctx = jnp.concatenate([q_nope, q_pe], axis=-1)
    k_ctx = jnp.concatenate([kc_nope, bcast_heads(kpe_ctx)], axis=-1)
    o = joint_attention([q_lat, q_ctx], [k_lat, k_ctx], [v, vc], MLA_SCALE)
    return merge_heads(o) @ w_out


def modulation(cond, w_ada, b_ada):
    m = jax.nn.silu(cond) @ w_ada + b_ada
    return m.reshape(cond.shape[0], 6, D_MODEL)


def modulate(x, g, shift, scale):
    return rms_norm(x, g) * (1 + scale[:, None, :]) + shift[:, None, :]


def gated_residual(x, y, g, gate):
    return x + gate[:, None, :] * rms_norm(y, g)


def channel_mixer(h, w1, w2):
    return jnp.square(jax.nn.relu(h @ w1)) @ w2


def setup_inputs(seed: int = 0) -> dict:
    key = jax.random.key(seed)
    ks = iter(jax.random.split(key, 40))
    f32 = jnp.float32

    def nrm(shape, scale=1.0):
        return jax.random.normal(next(ks), shape, f32) * scale

    base_decay = -5.0 - jnp.arange(H_RET, dtype=f32)
    return {
        'x_prompt': nrm((BATCH, SEQ, D_MODEL)),
        'x_sample': nrm((DEC_BATCH, DEC_SEQ, D_MODEL)),
        'state_ret_fwd': nrm((DEC_BATCH, N_A_LAYERS, H_RET, RET_DK, RET_DV), 0.5),
        'state_ret_bwd': nrm((DEC_BATCH, N_A_LAYERS, H_RET, RET_DK, RET_DV), 0.5),
        'cache_na_k': nrm((DEC_BATCH, N_A_LAYERS, H_NA, PAST_LEN, HEAD_DIM)),
        'cache_na_v': nrm((DEC_BATCH, N_A_LAYERS, H_NA, PAST_LEN, HEAD_DIM)),
        'cache_mla_ckv': nrm((DEC_BATCH, N_C_LAYERS, PAST_LEN, MLA_KV_LORA)),
        'cache_mla_kpe': nrm((DEC_BATCH, N_C_LAYERS, PAST_LEN, MLA_ROPE)),
        'c': nrm((DEC_BATCH, D_MODEL)),
        'c_ctx': nrm((D_MODEL,)),
        'w_ada': nrm((DEPTH, D_MODEL, 6 * D_MODEL), 0.5 * D_MODEL ** -0.5),
        'b_ada': nrm((DEPTH, 6 * D_MODEL), 0.1),
        'norm_gains': 1.0 + nrm((DEPTH, 4, D_MODEL), 0.05),
        'w_mlp_in': nrm((DEPTH, D_MODEL, D_FF), D_MODEL ** -0.5),
        'w_mlp_out': nrm((DEPTH, D_FF, D_MODEL), D_FF ** -0.5),
        'w_in_ac': nrm((N_A_LAYERS, D_MODEL, AC_IN), D_MODEL ** -0.5),
        'w_out_ac': nrm((N_A_LAYERS, AC_OUT, D_MODEL), AC_OUT ** -0.5),
        'ret_decay_fwd': base_decay + nrm((N_A_LAYERS, H_RET), 0.1),
        'ret_decay_bwd': base_decay + nrm((N_A_LAYERS, H_RET), 0.1),
        'na_rpb': nrm((N_A_LAYERS, H_NA, 2 * WIN_R - 1, 2 * WIN_C - 1), 0.1),
        'w_in_c': nrm((N_C_LAYERS, D_MODEL, C_IN), D_MODEL ** -0.5),
        'mla_q_norm': 1.0 + nrm((N_C_LAYERS, MLA_Q_LORA), 0.05),
        'mla_kv_norm': 1.0 + nrm((N_C_LAYERS, MLA_KV_LORA), 0.05),
        'w_uq': nrm((N_C_LAYERS, MLA_Q_LORA, MLA_HEADS * (MLA_NOPE + MLA_ROPE)), MLA_Q_LORA ** -0.5),
        'w_uk': nrm((N_C_LAYERS, MLA_KV_LORA, MLA_HEADS * MLA_NOPE), MLA_KV_LORA ** -0.5),
        'w_uv': nrm((N_C_LAYERS, MLA_KV_LORA, MLA_HEADS * MLA_V), MLA_KV_LORA ** -0.5),
        'w_out_c': nrm((N_C_LAYERS, MLA_HEADS * MLA_V, D_MODEL), (MLA_HEADS * MLA_V) ** -0.5),
    }


def reference(x_prompt, x_sample, state_ret_fwd, state_ret_bwd, cache_na_k, cache_na_v,
              cache_mla_ckv, cache_mla_kpe, c, c_ctx, w_ada, b_ada, norm_gains, w_mlp_in, w_mlp_out,
              w_in_ac, w_out_ac, ret_decay_fwd, ret_decay_bwd, na_rpb, w_in_c, mla_q_norm,
              mla_kv_norm, w_uq, w_uk, w_uv, w_out_c):
    xp, xs = x_prompt, x_sample
    ret_f, ret_b, na_k, na_v, mla_ckv, mla_kpe = [], [], [], [], [], []
    for layer in range(DEPTH):
        mc = modulation(c_ctx[None], w_ada[layer], b_ada[layer])
        ml = modulation(c, w_ada[layer], b_ada[layer])
        g = norm_gains[layer]
        hp = modulate(xp, g[0], mc[:, 0], mc[:, 1])
        hs = modulate(xs, g[0], ml[:, 0], ml[:, 1])
        if layer % 2 == 0:
            ia = layer // 2
            lg_f = log_decay(ret_decay_fwd[ia])
            lg_b = log_decay(ret_decay_bwd[ia])
            yp, s_f, s_b, nk, nv = mixer_ac_context(hp, w_in_ac[ia], w_out_ac[ia], lg_f, lg_b)
            ret_f.append(s_f)
            ret_b.append(s_b)
            na_k.append(nk)
            na_v.append(nv)
            ys = mixer_ac_latent(hs, w_in_ac[ia], w_out_ac[ia], lg_f, lg_b,
                                 state_ret_fwd[:, ia], state_ret_bwd[:, ia],
                                 cache_na_k[:, ia], cache_na_v[:, ia], na_rpb[ia])
        else:
            ic = layer // 2
            yp, ckv, kpe = mla_context(hp, w_in_c[ic], mla_q_norm[ic], mla_kv_norm[ic],
                                       w_uq[ic], w_uk[ic], w_uv[ic], w_out_c[ic])
            mla_ckv.append(ckv)
            mla_kpe.append(kpe)
            ys = mla_latent(hs, w_in_c[ic], mla_q_norm[ic], mla_kv_norm[ic], w_uq[ic], w_uk[ic],
                            w_uv[ic], w_out_c[ic], cache_mla_ckv[:, ic], cache_mla_kpe[:, ic])
        xp = gated_residual(xp, yp, g[1], mc[:, 2])
        xs = gated_residual(xs, ys, g[1], ml[:, 2])
        hp = modulate(xp, g[2], mc[:, 3], mc[:, 4])
        hs = modulate(xs, g[2], ml[:, 3], ml[:, 4])
        xp = gated_residual(xp, channel_mixer(hp, w_mlp_in[layer], w_mlp_out[layer]), g[3], mc[:, 5])
        xs = gated_residual(xs, channel_mixer(hs, w_mlp_in[layer], w_mlp_out[layer]), g[3], ml[:, 5])
    new_ret_fwd = jnp.stack(ret_f, axis=1)
    new_ret_bwd = jnp.stack(ret_b, axis=1)
    new_na_k = jnp.stack(na_k, axis=1)
    new_na_v = jnp.stack(na_v, axis=1)
    new_mla_ckv = jnp.stack(mla_ckv, axis=1)
    new_mla_kpe = jnp.stack(mla_kpe, axis=1)
    return (xp, xs, new_ret_fwd, new_ret_bwd, new_na_k, new_na_v, new_mla_ckv, new_mla_kpe)
```

```python
import functools

import numpy as np
import jax
import jax.numpy as jnp
from jax import lax
from jax.experimental import pallas as pl
from jax.experimental.pallas import tpu as pltpu

D_MODEL = 1024
DEPTH = 4
GRID_W = 64
HEAD_DIM = 64
N_HEADS = D_MODEL // HEAD_DIM
H_RET = N_HEADS // 2
H_NA = N_HEADS - H_RET
RET_CHUNK = 128
WIN_R = 8
WIN_C = 16
MLA_HEADS = N_HEADS
MLA_Q_LORA = 384
MLA_KV_LORA = 256
MLA_NOPE = 64
MLA_ROPE = 32
MLA_V = 64
MLA_SCALE = (MLA_NOPE + MLA_ROPE) ** -0.5
ROPE_BASE = 10000.0
D_FF = 4 * D_MODEL
EPS = 1e-6
NEG_INF = -1e30

LANES = 128
HEAD_PAIR = 2 * HEAD_DIM
MLA_QK_PAD = 128
VMEM_LIMIT = 56 * 1024 * 1024

F32 = jnp.float32
BF16 = jnp.bfloat16


def _cparams(*sem):
    return pltpu.CompilerParams(dimension_semantics=sem, vmem_limit_bytes=VMEM_LIMIT)


def _dot(a, b):
    return jnp.dot(a, b, preferred_element_type=F32)


def _dot_nt(a, b):
    return lax.dot_general(a, b, (((1,), (1,)), ((), ())), preferred_element_type=F32)


def _dot_tn(a, b):
    return lax.dot_general(a, b, (((0,), (0,)), ((), ())), preferred_element_type=F32)


def _rms(x, g):
    ms = jnp.mean(x * x, axis=-1, keepdims=True)
    return x * lax.rsqrt(ms + EPS) * g


def _modulate(x, g, shift, scale):
    return _rms(x, g) * (1.0 + scale) + shift


def _mod_kernel(c_ref, w_ref, b_ref, o_ref):
    c = c_ref[...]
    s = c / (1.0 + jnp.exp(-c))
    o_ref[0] = _dot(s.astype(BF16), w_ref[0].astype(BF16)) + b_ref[0]


def _modulation_all(cond, w_ada, b_ada):
    r = cond.shape[0]
    tn = 1536
    out = pl.pallas_call(
        _mod_kernel,
        grid=(DEPTH, 6 * D_MODEL // tn),
        in_specs=[
            pl.BlockSpec((r, D_MODEL), lambda l, j: (0, 0)),
            pl.BlockSpec((1, D_MODEL, tn), lambda l, j: (l, 0, j)),
            pl.BlockSpec((1, 1, tn), lambda l, j: (l, 0, j)),
        ],
        out_specs=pl.BlockSpec((1, r, tn), lambda l, j: (l, 0, j)),
        out_shape=jax.ShapeDtypeStruct((DEPTH, r, 6 * D_MODEL), F32),
        compiler_params=_cparams("parallel", "parallel"),
        name="modulation",
    )(cond, w_ada, b_ada.reshape(DEPTH, 1, 6 * D_MODEL))
    return out.reshape(DEPTH, r, 6, D_MODEL)


def _mod_spec(tiles_per_row):
    return pl.BlockSpec((1, 6, D_MODEL), lambda i: (i // tiles_per_row, 0, 0))


def _proj_even_kernel(x_ref, g_ref, mod_ref, w_ref, *out_refs, mults):
    m = mod_ref[0]
    h = _modulate(x_ref[...], g_ref[...], m[0:1], m[1:2]).astype(BF16)
    width = out_refs[0].shape[-1]
    for i, (o_ref, mult) in enumerate(zip(out_refs, mults)):
        y = _dot(h, w_ref[:, i * width:(i + 1) * width])
        if mult != 1.0:
            y = y * mult
        o_ref[...] = y.astype(o_ref.dtype)


def _proj_even(x, g, mod, w, tokens_per_row, kv_dtype):
    t = x.shape[0]
    tm = min(512, t)
    width = H_RET * HEAD_DIM
    dtypes = [BF16, BF16, BF16, F32, BF16, kv_dtype, kv_dtype]
    mults = (1.0, HEAD_DIM ** -0.5, 1.0, 1.0, 1.0, 1.0, 1.0)
    return pl.pallas_call(
        functools.partial(_proj_even_kernel, mults=mults),
        grid=(t // tm,),
        in_specs=[
            pl.BlockSpec((tm, D_MODEL), lambda i: (i, 0)),
            pl.BlockSpec((1, D_MODEL), lambda i: (0, 0)),
            _mod_spec(tokens_per_row // tm),
            pl.BlockSpec(w.shape, lambda i: (0, 0)),
        ],
        out_specs=[pl.BlockSpec((tm, width), lambda i: (i, 0)) for _ in dtypes],
        out_shape=[jax.ShapeDtypeStruct((t, width), dt) for dt in dtypes],
        compiler_params=_cparams("parallel"),
        name="proj_even",
    )(x, g, mod, w)


def _retention_kernel(lgf_ref, lgb_ref, q_ref, k_ref, v_ref, g_ref, *rest, n_chunks, has_state, emit_state):
    if has_state:
        sf0_ref, sb0_ref = rest[0], rest[1]
        rest = rest[2:]
    o_ref = rest[0]
    rest = rest[1:]
    if emit_state:
        sfo_ref, sbo_ref = rest[0], rest[1]
        rest = rest[2:]
    acc_ref = rest[0]

    c = RET_CHUNK
    pair = pl.program_id(1)
    row = lax.broadcasted_iota(jnp.int32, (c, c), 0).astype(F32)
    col = lax.broadcasted_iota(jnp.int32, (c, c), 1).astype(F32)
    diff = row - col
    pos = lax.broadcasted_iota(jnp.int32, (c, 1), 0).astype(F32)

    for j in range(2):
        lgf = lgf_ref[2 * pair + j]
        lgb = lgb_ref[2 * pair + j]
        mask = (jnp.where(diff >= 0, jnp.exp(lgf * jnp.maximum(diff, 0.0)), 0.0)
                + jnp.where(diff <= 0, jnp.exp(lgb * jnp.maximum(-diff, 0.0)), 0.0))
        kdec_f = jnp.exp(lgf * (c - 1 - pos))
        kdec_b = jnp.exp(lgb * pos)
        qdec_f = jnp.exp(lgf * (pos + 1.0))
        qdec_b = jnp.exp(lgb * (c - pos))
        cdec_f = jnp.exp(jnp.full((1, 1), lgf * c, F32))
        cdec_b = jnp.exp(jnp.full((1, 1), lgb * c, F32))
        lanes = slice(j * HEAD_DIM, (j + 1) * HEAD_DIM)

        def load(n):
            rows = pl.ds(pl.multiple_of(n * c, c), c)
            return q_ref[0, rows, lanes], k_ref[0, rows, lanes], v_ref[0, rows, lanes], rows

        def fwd(n, state):
            q, k, v, rows = load(n)
            s = _dot_nt(q, k) * mask
            o = _dot(s.astype(BF16), v)
            qd = (q.astype(F32) * qdec_f).astype(BF16)
            o = o + _dot(qd, state.astype(BF16))
            kd = (k.astype(F32) * kdec_f).astype(BF16)
            acc_ref[rows, lanes] = o
            return cdec_f * state + _dot_tn(kd, v)

        def bwd(i, state):
            n = n_chunks - 1 - i
            q, k, v, rows = load(n)
            qd = (q.astype(F32) * qdec_b).astype(BF16)
            acc_ref[rows, lanes] += _dot(qd, state.astype(BF16))
            kd = (k.astype(F32) * kdec_b).astype(BF16)
            return cdec_b * state + _dot_tn(kd, v)

        if has_state:
            sf0, sb0 = sf0_ref[0, 0, j], sb0_ref[0, 0, j]
        else:
            sf0 = sb0 = jnp.zeros((HEAD_DIM, HEAD_DIM), F32)
        fin_f = lax.fori_loop(0, n_chunks, fwd, sf0)
        fin_b = lax.fori_loop(0, n_chunks, bwd, sb0)
        if emit_state:
            sfo_ref[0, j] = fin_f
            sbo_ref[0, j] = fin_b

    def finish(n, carry):
        rows = pl.ds(pl.multiple_of(n * c, c), c)
        o = acc_ref[rows, :]
        parts = []
        for j in range(2):
            oj = o[:, j * HEAD_DIM:(j + 1) * HEAD_DIM]
            mu = jnp.mean(oj, axis=-1, keepdims=True)
            var = jnp.mean(jnp.square(oj - mu), axis=-1, keepdims=True)
            parts.append((oj - mu) * lax.rsqrt(var + EPS))
        gate = g_ref[0, rows, :]
        on = jnp.concatenate(parts, axis=-1)
        o_ref[0, rows, :] = (on * (gate / (1.0 + jnp.exp(-gate)))).astype(o_ref.dtype)
        return carry

    lax.fori_loop(0, n_chunks, finish, 0)


def _retention(q, k, v, g, lg_f, lg_b, state_f, state_b, ia, emit_state):
    b, l, _ = q.shape
    has_state = state_f is not None
    blk = lambda: pl.BlockSpec((1, l, HEAD_PAIR), lambda bi, p: (bi, 0, p))
    smem = pl.BlockSpec(memory_space=pltpu.SMEM)
    in_specs = [smem, smem, blk(), blk(), blk(), blk()]
    args = [lg_f, lg_b, q, k, v, g]
    if has_state:
        st = lambda: pl.BlockSpec((1, 1, 2, HEAD_DIM, HEAD_DIM), lambda bi, p: (bi, ia, p, 0, 0))
        in_specs += [st(), st()]
        args += [state_f, state_b]
    out_specs = [blk()]
    out_shape = [jax.ShapeDtypeStruct((b, l, H_RET * HEAD_DIM), BF16)]
    if emit_state:
        so = lambda: pl.BlockSpec((1, 2, HEAD_DIM, HEAD_DIM), lambda bi, p: (bi, p, 0, 0))
        out_specs += [so(), so()]
        out_shape += [jax.ShapeDtypeStruct((b, H_RET, HEAD_DIM, HEAD_DIM), F32)] * 2
    return pl.pallas_call(
        functools.partial(_retention_kernel, n_chunks=l // RET_CHUNK, has_state=has_state,
                          emit_state=emit_state),
        grid=(b, H_RET // 2),
        in_specs=in_specs,
        out_specs=out_specs,
        out_shape=out_shape,
        scratch_shapes=[pltpu.VMEM((l, HEAD_PAIR), F32)],
        compiler_params=_cparams("parallel", "parallel"),
        name="retention",
    )(*args)


def _attn_kernel(*refs, nseg, dq, scale, emit_kv):
    q_refs = refs[0:nseg]
    k_refs = refs[nseg:2 * nseg]
    v_refs = refs[2 * nseg:3 * nseg]
    o_ref = refs[3 * nseg]
    outs = []
    for j in range(2):
        ql = slice(j * dq, (j + 1) * dq)
        vl = slice(j * HEAD_DIM, (j + 1) * HEAD_DIM)
        scores = [_dot_nt(q_refs[s][0, :, ql], k_refs[s][0, :, ql].astype(BF16)) for s in range(nseg)]
        m = functools.reduce(jnp.maximum, [jnp.max(s, axis=-1, keepdims=True) for s in scores])
        acc = None
        den = None
        for s in range(nseg):
            p = jnp.exp((scores[s] - m) * scale)
            ps = jnp.sum(p, axis=-1, keepdims=True)
            pv = _dot(p.astype(BF16), v_refs[s][0, :, vl].astype(BF16))
            acc = pv if acc is None else acc + pv
            den = ps if den is None else den + ps
        outs.append(acc / den)
    o_ref[0] = jnp.concatenate(outs, axis=-1).astype(o_ref.dtype)
    if emit_kv:
        ko_ref, vo_ref = refs[3 * nseg + 1], refs[3 * nseg + 2]
        for j in range(2):
            ko_ref[0, j] = k_refs[0][0, :, j * dq:(j + 1) * dq]
            vo_ref[0, j] = v_refs[0][0, :, j * HEAD_DIM:(j + 1) * HEAD_DIM]


def _attention(qs, ks, vs, n_heads, dq, scale, tq, emit_kv=False):
    nseg = len(qs)
    b, lq, _ = qs[0].shape
    in_specs = [pl.BlockSpec((1, tq, 2 * dq), lambda bi, p, qi: (bi, qi, p)) for _ in qs]
    in_specs += [pl.BlockSpec((1, k.shape[1], 2 * dq), lambda bi, p, qi: (bi, 0, p)) for k in ks]
    in_specs += [pl.BlockSpec((1, v.shape[1], HEAD_PAIR), lambda bi, p, qi: (bi, 0, p)) for v in vs]
    out_specs = [pl.BlockSpec((1, tq, HEAD_PAIR), lambda bi, p, qi: (bi, qi, p))]
    out_shape = [jax.ShapeDtypeStruct((b, lq, n_heads * HEAD_DIM), BF16)]
    if emit_kv:
        assert nseg == 1 and tq == lq and dq == HEAD_DIM
        lk = ks[0].shape[1]
        kv = lambda: pl.BlockSpec((1, 2, lk, HEAD_DIM), lambda bi, p, qi: (bi, p, 0, 0))
        out_specs += [kv(), kv()]
        out_shape += [jax.ShapeDtypeStruct((b, n_heads, lk, HEAD_DIM), F32)] * 2
    res = pl.pallas_call(
        functools.partial(_attn_kernel, nseg=nseg, dq=dq, scale=scale, emit_kv=emit_kv),
        grid=(b, n_heads // 2, lq // tq),
        in_specs=in_specs,
        out_specs=out_specs,
        out_shape=out_shape,
        compiler_params=_cparams("parallel", "parallel", "arbitrary"),
        name="attention",
    )(*qs, *ks, *vs)
    return res if emit_kv else res[0]


def _na_bias_tables(rpb):
    qc = np.arange(GRID_W)
    kc = np.arange(GRID_W)
    win_start = np.clip(qc - WIN_C // 2, 0, GRID_W - WIN_C)
    valid = (kc[None, :] >= win_start[:, None]) & (kc[None, :] < win_start[:, None] + WIN_C)
    dc = np.clip(kc[None, :] - qc[:, None], -(WIN_C - 1), WIN_C - 1) + (WIN_C - 1)
    d = np.arange(WIN_R)
    j = np.arange(WIN_R)
    dr = j[None, :] + (WIN_R - 1) - d[:, None]
    bias = rpb[:, dr[:, None, :, None], dc[None, :, None, :]].astype(F32)
    bias = jnp.where(valid[None, None, :, None, :], bias, NEG_INF)
    return bias.reshape(rpb.shape[0], WIN_R, GRID_W, WIN_R * GRID_W)


def _na_latent_kernel(q_ref, k_ref, v_ref, kc_ref, vc_ref, bias_ref, o_ref, *, rows):
    scale = HEAD_DIM ** -0.5
    w = GRID_W

    def body(r, carry):
        rs = jnp.clip(r - WIN_R // 2, 0, rows - WIN_R)
        d = r - rs
        qrows = pl.ds(pl.multiple_of(r * w, w), w)
        krows = pl.ds(pl.multiple_of(rs * w, w), WIN_R * w)
        outs = []
        for j in range(2):
            lanes = slice(j * HEAD_DIM, (j + 1) * HEAD_DIM)
            q = q_ref[0, qrows, lanes]
            s_loc = _dot_nt(q, k_ref[0, krows, lanes]) * scale + bias_ref[j, d]
            s_ctx = _dot_nt(q, kc_ref[0, 0, j].astype(BF16)) * scale
            m = jnp.maximum(jnp.max(s_loc, axis=-1, keepdims=True), jnp.max(s_ctx, axis=-1, keepdims=True))
            p_loc = jnp.exp(s_loc - m)
            p_ctx = jnp.exp(s_ctx - m)
            den = jnp.sum(p_loc, axis=-1, keepdims=True) + jnp.sum(p_ctx, axis=-1, keepdims=True)
            acc = (_dot(p_loc.astype(BF16), v_ref[0, krows, lanes])
                   + _dot(p_ctx.astype(BF16), vc_ref[0, 0, j].astype(BF16)))
            outs.append(acc / den)
        o_ref[0, qrows, :] = jnp.concatenate(outs, axis=-1).astype(o_ref.dtype)
        return carry

    lax.fori_loop(0, rows, body, 0)


def _na_latent(q, k, v, cache_k, cache_v, ia, bias):
    b, l, _ = q.shape
    lc = cache_k.shape[3]
    blk = lambda: pl.BlockSpec((1, l, HEAD_PAIR), lambda p, bi: (bi, 0, p))
    ctx = lambda: pl.BlockSpec((1, 1, 2, lc, HEAD_DIM), lambda p, bi: (bi, ia, p, 0, 0))
    return pl.pallas_call(
        functools.partial(_na_latent_kernel, rows=l // GRID_W),
        grid=(H_NA // 2, b),
        in_specs=[blk(), blk(), blk(), ctx(), ctx(),
                  pl.BlockSpec((2,) + bias.shape[1:], lambda p, bi: (p, 0, 0, 0))],
        out_specs=blk(),
        out_shape=jax.ShapeDtypeStruct((b, l, H_NA * HEAD_DIM), BF16),
        compiler_params=_cparams("parallel", "parallel"),
        name="na_latent",
    )(q, k, v, cache_k, cache_v, bias)


def _outproj_kernel(*refs, n_in, gate_i):
    a_refs = refs[:n_in]
    w_ref, x_ref, g_ref, mod_ref, o_ref = refs[n_in:]
    y = None
    k0 = 0
    for a_ref in a_refs:
        kw = a_ref.shape[-1]
        part = _dot(a_ref[...], w_ref[k0:k0 + kw, :])
        y = part if y is None else y + part
        k0 += kw
    m = mod_ref[0]
    o_ref[...] = x_ref[...] + m[gate_i:gate_i + 1] * _rms(y, g_ref[...])


def _outproj_residual(acts, w, x, g, mod, tokens_per_row, gate_i):
    t = x.shape[0]
    tm = min(512, t)
    in_specs = [pl.BlockSpec((tm, a.shape[1]), lambda i: (i, 0)) for a in acts]
    in_specs += [
        pl.BlockSpec(w.shape, lambda i: (0, 0)),
        pl.BlockSpec((tm, D_MODEL), lambda i: (i, 0)),
        pl.BlockSpec((1, D_MODEL), lambda i: (0, 0)),
        _mod_spec(tokens_per_row // tm),
    ]
    return pl.pallas_call(
        functools.partial(_outproj_kernel, n_in=len(acts), gate_i=gate_i),
        grid=(t // tm,),
        in_specs=in_specs,
        out_specs=pl.BlockSpec((tm, D_MODEL), lambda i: (i, 0)),
        out_shape=jax.ShapeDtypeStruct((t, D_MODEL), F32),
        compiler_params=_cparams("parallel"),
        name="outproj_residual",
    )(*acts, w, x, g, mod)


def _mlp_kernel(x_ref, gin_ref, gout_ref, mod_ref, w1_ref, w2_ref, o_ref, h_ref, acc_ref):
    f = pl.program_id(1)

    @pl.when(f == 0)
    def _():
        m = mod_ref[0]
        h_ref[...] = _modulate(x_ref[...], gin_ref[...], m[3:4], m[4:5]).astype(BF16)

    a = jnp.maximum(_dot(h_ref[...], w1_ref[...]), 0.0)
    part = _dot((a * a).astype(BF16), w2_ref[...])

    @pl.when(f == 0)
    def _():
        acc_ref[...] = part

    @pl.when(f > 0)
    def _():
        acc_ref[...] += part

    @pl.when(f == pl.num_programs(1) - 1)
    def _():
        m = mod_ref[0]
        o_ref[...] = x_ref[...] + m[5:6] * _rms(acc_ref[...], gout_ref[...])


def _mlp_residual(x, g_in, g_out, mod, w1, w2, tokens_per_row):
    t = x.shape[0]
    tm = min(1024, t)
    tf = 1024
    return pl.pallas_call(
        _mlp_kernel,
        grid=(t // tm, D_FF // tf),
        in_specs=[
            pl.BlockSpec((tm, D_MODEL), lambda i, f: (i, 0)),
            pl.BlockSpec((1, D_MODEL), lambda i, f: (0, 0)),
            pl.BlockSpec((1, D_MODEL), lambda i, f: (0, 0)),
            pl.BlockSpec((1, 6, D_MODEL), lambda i, f: (i // (tokens_per_row // tm), 0, 0)),
            pl.BlockSpec((D_MODEL, tf), lambda i, f: (0, f)),
            pl.BlockSpec((tf, D_MODEL), lambda i, f: (f, 0)),
        ],
        out_specs=pl.BlockSpec((tm, D_MODEL), lambda i, f: (i, 0)),
        out_shape=jax.ShapeDtypeStruct((t, D_MODEL), F32),
        scratch_shapes=[pltpu.VMEM((tm, D_MODEL), BF16), pltpu.VMEM((tm, D_MODEL), F32)],
        compiler_params=_cparams("parallel", "arbitrary"),
        name="mlp_residual",
    )(x, g_in, g_out, mod, w1, w2)


def _rope_lanes(y, table, keep):
    lane = lax.broadcasted_iota(jnp.int32, y.shape, 1)
    prod = y * table
    rot = prod + pltpu.roll(prod, LANES - MLA_ROPE, 1)
    return jnp.where((lane >= MLA_NOPE) & (lane < MLA_NOPE + MLA_ROPE), rot, keep)


def _proj_mla_kernel(*refs, rope):
    (x_ref, g_ref, mod_ref, win_ref, qn_ref, kvn_ref, wuq_ref, wuk_ref, wuv_ref) = refs[:9]
    refs = refs[9:]
    if rope:
        tab_ref, refs = refs[0], refs[1:]
        qlat_ref, refs = refs[0], refs[1:]
    qctx_ref, k_ref, v_ref, ckv_ref, kpe_ref = refs

    m = mod_ref[0]
    h = _modulate(x_ref[...], g_ref[...], m[0:1], m[1:2]).astype(BF16)
    y = _dot(h, win_ref[...])
    c_q = y[:, :MLA_Q_LORA]
    c_kv = _rms(y[:, MLA_Q_LORA:MLA_Q_LORA + MLA_KV_LORA], kvn_ref[...])
    pe_blk = y[:, MLA_Q_LORA + MLA_KV_LORA:]
    ckv_ref[...] = c_kv
    kpe_ref[...] = pe_blk[:, MLA_NOPE:MLA_NOPE + MLA_ROPE]

    lane = lax.broadcasted_iota(jnp.int32, pe_blk.shape, 1)
    if rope:
        table = tab_ref[...]
        k_pe = _rope_lanes(pe_blk, table, jnp.zeros_like(pe_blk))
    else:
        k_pe = jnp.where((lane >= MLA_NOPE) & (lane < MLA_NOPE + MLA_ROPE), pe_blk, 0.0)

    q = _dot(_rms(c_q, qn_ref[...]).astype(BF16), wuq_ref[...])
    c_kv_b = c_kv.astype(BF16)
    kn = _dot(c_kv_b, wuk_ref[...])
    v_ref[...] = _dot(c_kv_b, wuv_ref[...]).astype(v_ref.dtype)
    for hd in range(MLA_HEADS):
        lanes = slice(hd * MLA_QK_PAD, (hd + 1) * MLA_QK_PAD)
        qh = q[:, lanes]
        qctx_ref[:, lanes] = qh.astype(qctx_ref.dtype)
        if rope:
            qlat_ref[:, lanes] = _rope_lanes(qh, table, qh).astype(qlat_ref.dtype)
        k_ref[:, lanes] = (kn[:, lanes] + k_pe).astype(k_ref.dtype)


def _proj_mla(x, g, mod, w_in, q_norm, kv_norm, w_uq, w_uk, w_uv, tokens_per_row, rope_table):
    t = x.shape[0]
    tm = min(512, t)
    rope = rope_table is not None
    full = lambda a: pl.BlockSpec(a.shape, lambda i: (0,) * a.ndim)
    tok = lambda n: pl.BlockSpec((tm, n), lambda i: (i, 0))
    in_specs = [tok(D_MODEL), full(g), _mod_spec(tokens_per_row // tm), full(w_in), full(q_norm),
                full(kv_norm), full(w_uq), full(w_uk), full(w_uv)]
    args = [x, g, mod, w_in, q_norm, kv_norm, w_uq, w_uk, w_uv]
    qk = MLA_HEADS * MLA_QK_PAD
    out_specs, out_shape = [], []
    if rope:
        tiles_per_seq = rope_table.shape[0] // tm
        in_specs.append(pl.BlockSpec((tm, LANES), lambda i: (i % tiles_per_seq, 0)))
        args.append(rope_table)
        out_specs.append(tok(qk))
        out_shape.append(jax.ShapeDtypeStruct((t, qk), BF16))
    out_specs += [tok(qk), tok(qk), tok(MLA_HEADS * MLA_V), tok(MLA_KV_LORA), tok(MLA_ROPE)]
    out_shape += [jax.ShapeDtypeStruct((t, qk), BF16), jax.ShapeDtypeStruct((t, qk), BF16),
                  jax.ShapeDtypeStruct((t, MLA_HEADS * MLA_V), BF16),
                  jax.ShapeDtypeStruct((t, MLA_KV_LORA), F32), jax.ShapeDtypeStruct((t, MLA_ROPE), F32)]
    return pl.pallas_call(
        functools.partial(_proj_mla_kernel, rope=rope),
        grid=(t // tm,),
        in_specs=in_specs,
        out_specs=out_specs,
        out_shape=out_shape,
        compiler_params=_cparams("parallel"),
        name="proj_mla",
    )(*args)


def _ctx_kv_kernel(ckv_ref, kpe_ref, wuk_ref, wuv_ref, place_ref, k_ref, v_ref):
    c = ckv_ref[...].astype(BF16)
    kn = _dot(c, wuk_ref[...])
    k_pe = _dot(kpe_ref[...].astype(BF16), place_ref[...])
    v_ref[...] = _dot(c, wuv_ref[...]).astype(v_ref.dtype)
    for hd in range(MLA_HEADS):
        lanes = slice(hd * MLA_QK_PAD, (hd + 1) * MLA_QK_PAD)
        k_ref[:, lanes] = (kn[:, lanes] + k_pe).astype(k_ref.dtype)


def _ctx_kv(ckv, kpe, w_uk, w_uv, place):
    t = ckv.shape[0]
    tm = min(512, t)
    full = lambda a: pl.BlockSpec(a.shape, lambda i: (0,) * a.ndim)
    tok = lambda n: pl.BlockSpec((tm, n), lambda i: (i, 0))
    return pl.pallas_call(
        _ctx_kv_kernel,
        grid=(t // tm,),
        in_specs=[tok(MLA_KV_LORA), tok(MLA_ROPE), full(w_uk), full(w_uv), full(place)],
        out_specs=[tok(MLA_HEADS * MLA_QK_PAD), tok(MLA_HEADS * MLA_V)],
        out_shape=[jax.ShapeDtypeStruct((t, MLA_HEADS * MLA_QK_PAD), BF16),
                   jax.ShapeDtypeStruct((t, MLA_HEADS * MLA_V), BF16)],
        compiler_params=_cparams("parallel"),
        name="mla_ctx_kv",
    )(ckv, kpe, w_uk, w_uv, place)


def _pair_swap(w):
    return w.reshape(w.shape[:-1] + (w.shape[-1] // 2, 2))[..., ::-1].reshape(w.shape)


def _mla_weights(w_in, w_uq, w_uk):
    k_pe_cols = w_in[:, MLA_Q_LORA + MLA_KV_LORA:]
    w_in_ext = jnp.concatenate(
        [w_in[:, :MLA_Q_LORA + MLA_KV_LORA], jnp.zeros((D_MODEL, MLA_NOPE), w_in.dtype),
         k_pe_cols, _pair_swap(k_pe_cols)], axis=1).astype(BF16)
    uq = w_uq.reshape(MLA_Q_LORA, MLA_HEADS, MLA_NOPE + MLA_ROPE)
    uq_ext = jnp.concatenate([uq, _pair_swap(uq[..., MLA_NOPE:])], axis=-1)
    uq_ext = uq_ext.reshape(MLA_Q_LORA, MLA_HEADS * MLA_QK_PAD).astype(BF16)
    uk = w_uk.reshape(MLA_KV_LORA, MLA_HEADS, MLA_NOPE)
    uk_ext = jnp.concatenate([uk, jnp.zeros((MLA_KV_LORA, MLA_HEADS, MLA_QK_PAD - MLA_NOPE), uk.dtype)], axis=-1)
    uk_ext = uk_ext.reshape(MLA_KV_LORA, MLA_HEADS * MLA_QK_PAD).astype(BF16)
    return w_in_ext, uq_ext, uk_ext


def _rope_table(l):
    t = jnp.arange(l)
    row = (t // GRID_W).astype(F32)
    col = (t % GRID_W).astype(F32)
    nf = MLA_ROPE // 4
    inv = ROPE_BASE ** (-jnp.arange(nf, dtype=F32) / nf)
    ang = jnp.concatenate([row[:, None] * inv, col[:, None] * inv], axis=-1)
    cos, sin = jnp.cos(ang), jnp.sin(ang)
    cc = jnp.repeat(cos, 2, axis=-1)
    ss = jnp.stack([-sin, sin], axis=-1).reshape(l, MLA_ROPE)
    return jnp.concatenate([jnp.ones((l, MLA_NOPE), F32), cc, ss], axis=-1)


def kernel(x_prompt, x_sample, state_ret_fwd, state_ret_bwd, cache_na_k, cache_na_v, cache_mla_ckv,
           cache_mla_kpe, c, c_ctx, w_ada, b_ada, norm_gains, w_mlp_in, w_mlp_out, w_in_ac, w_out_ac,
           ret_decay_fwd, ret_decay_bwd, na_rpb, w_in_c, mla_q_norm, mla_kv_norm, w_uq, w_uk, w_uv, w_out_c):
    bp, lp, _ = x_prompt.shape
    bs, ls, _ = x_sample.shape
    lc = cache_mla_ckv.shape[2]
    xp = x_prompt.reshape(bp * lp, D_MODEL)
    xs = x_sample.reshape(bs * ls, D_MODEL)

    n_cond = bs + 1
    cond = jnp.concatenate([c, c_ctx[None]], axis=0)
    mods = _modulation_all(cond, w_ada, b_ada)
    rope_table = _rope_table(ls)
    place = jnp.zeros((MLA_ROPE, MLA_QK_PAD), BF16).at[
        jnp.arange(MLA_ROPE), MLA_NOPE + jnp.arange(MLA_ROPE)].set(1.0)

    ret_f, ret_b, na_k, na_v, mla_ckv, mla_kpe = [], [], [], [], [], []
    for layer in range(DEPTH):
        mod_s = mods[layer, :bs]
        mod_p = mods[layer, bs:n_cond]
        g = norm_gains[layer]
        g0, g1, g2, g3 = (g[i:i + 1] for i in range(4))
        tp, ts = bp * lp, ls
        if layer % 2 == 0:
            ia = layer // 2
            w_in = w_in_ac[ia].astype(BF16)
            w_out = w_out_ac[ia].astype(BF16)
            lg_f = jnp.log1p(-jnp.exp2(ret_decay_fwd[ia].astype(F32)))
            lg_b = jnp.log1p(-jnp.exp2(ret_decay_bwd[ia].astype(F32)))
            seq = lambda a, b, l: a.reshape(b, l, a.shape[-1])

            rq, rk, rv, rg, nq, nk, nv = [seq(a, bp, lp) for a in _proj_even(xp, g0, mod_p, w_in, tp, F32)]
            o_ret, s_f, s_b = _retention(rq, rk, rv, rg, lg_f, lg_b, None, None, ia, True)
            o_na, k_out, v_out = _attention([nq], [nk], [nv], H_NA, HEAD_DIM, HEAD_DIM ** -0.5, lp, emit_kv=True)
            ret_f.append(s_f)
            ret_b.append(s_b)
            na_k.append(k_out)
            na_v.append(v_out)
            xp = _outproj_residual([o_ret.reshape(tp, -1), o_na.reshape(tp, -1)], w_out, xp, g1, mod_p, tp, 2)

            rq, rk, rv, rg, nq, nk, nv = [seq(a, bs, ls) for a in _proj_even(xs, g0, mod_s, w_in, ts, BF16)]
            o_ret = _retention(rq, rk, rv, rg, lg_f, lg_b, state_ret_fwd, state_ret_bwd, ia, False)[0]
            o_na = _na_latent(nq, nk, nv, cache_na_k, cache_na_v, ia, _na_bias_tables(na_rpb[ia]))
            xs = _outproj_residual([o_ret.reshape(bs * ls, -1), o_na.reshape(bs * ls, -1)], w_out, xs, g1,
                                   mod_s, ts, 2)
        else:
            ic = layer // 2
            w_in_ext, uq_ext, uk_ext = _mla_weights(w_in_c[ic], w_uq[ic], w_uk[ic])
            uv = w_uv[ic].astype(BF16)
            w_out = w_out_c[ic].astype(BF16)
            qn, kvn = mla_q_norm[ic][None], mla_kv_norm[ic][None]

            q, k, v, ckv, kpe = _proj_mla(xp, g0, mod_p, w_in_ext, qn, kvn, uq_ext, uk_ext, uv, tp, None)
            mla_ckv.append(ckv.reshape(bp, lp, MLA_KV_LORA))
            mla_kpe.append(kpe.reshape(bp, lp, MLA_ROPE))
            sq = lambda a, b, l: a.reshape(b, l, a.shape[-1])
            o = _attention([sq(q, bp, lp)], [sq(k, bp, lp)], [sq(v, bp, lp)], MLA_HEADS, MLA_QK_PAD,
                           MLA_SCALE, lp)
            xp = _outproj_residual([o.reshape(tp, -1)], w_out, xp, g1, mod_p, tp, 2)

            q_lat, q_ctx, k, v, _, _ = _proj_mla(xs, g0, mod_s, w_in_ext, qn, kvn, uq_ext, uk_ext, uv, ts,
                                                 rope_table)
            k_c, v_c = _ctx_kv(cache_mla_ckv[:, ic].reshape(bs * lc, MLA_KV_LORA),
                               cache_mla_kpe[:, ic].reshape(bs * lc, MLA_ROPE), uk_ext, uv, place)
            o = _attention([sq(q_lat, bs, ls), sq(q_ctx, bs, ls)], [sq(k, bs, ls), sq(k_c, bs, lc)],
                           [sq(v, bs, ls), sq(v_c, bs, lc)], MLA_HEADS, MLA_QK_PAD, MLA_SCALE, 256)
            xs = _outproj_residual([o.reshape(bs * ls, -1)], w_out, xs, g1, mod_s, ts, 2)

        w1 = w_mlp_in[layer].astype(BF16)
        w2 = w_mlp_out[layer].astype(BF16)
        xp = _mlp_residual(xp, g2, g3, mod_p, w1, w2, tp)
        xs = _mlp_residual(xs, g2, g3, mod_s, w1, w2, ts)

    return (xp.reshape(bp, lp, D_MODEL), xs.reshape(bs, ls, D_MODEL),
            jnp.stack(ret_f, axis=1), jnp.stack(ret_b, axis=1),
            jnp.stack(na_k, axis=1), jnp.stack(na_v, axis=1),
            jnp.stack(mla_ckv, axis=1), jnp.stack(mla_kpe, axis=1))
```

```python
import functools

import numpy as np
import jax
import jax.numpy as jnp
from jax import lax
from jax.experimental import pallas as pl
from jax.experimental.pallas import tpu as pltpu

D_MODEL = 1024
DEPTH = 4
GRID_W = 64
HEAD_DIM = 64
N_HEADS = D_MODEL // HEAD_DIM
H_RET = N_HEADS // 2
H_NA = N_HEADS - H_RET
RET_CHUNK = 128
WIN_R = 8
WIN_C = 16
MLA_HEADS = N_HEADS
MLA_Q_LORA = 384
MLA_KV_LORA = 256
MLA_NOPE = 64
MLA_ROPE = 32
MLA_V = 64
MLA_SCALE = (MLA_NOPE + MLA_ROPE) ** -0.5
ROPE_BASE = 10000.0
D_FF = 4 * D_MODEL
EPS = 1e-6
NEG_INF = -1e30

LANES = 128
HEAD_PAIR = 2 * HEAD_DIM
MLA_QK_PAD = 128
VMEM_LIMIT = 56 * 1024 * 1024

F32 = jnp.float32
BF16 = jnp.bfloat16


def _cparams(*sem):
    return pltpu.CompilerParams(dimension_semantics=sem, vmem_limit_bytes=VMEM_LIMIT)


def _dot(a, b):
    return jnp.dot(a, b, preferred_element_type=F32)


def _dot_nt(a, b):
    return lax.dot_general(a, b, (((1,), (1,)), ((), ())), preferred_element_type=F32)


def _dot_tn(a, b):
    return lax.dot_general(a, b, (((0,), (0,)), ((), ())), preferred_element_type=F32)


def _rms(x, g):
    ms = jnp.mean(x * x, axis=-1, keepdims=True)
    return x * lax.rsqrt(ms + EPS) * g


def _modulate(x, g, shift, scale):
    return _rms(x, g) * (1.0 + scale) + shift


def _mod_kernel(c_ref, w_ref, b_ref, o_ref):
    c = c_ref[...]
    s = c / (1.0 + jnp.exp(-c))
    o_ref[0] = _dot(s.astype(BF16), w_ref[0].astype(BF16)) + b_ref[0]


def _modulation_all(cond, w_ada, b_ada):
    r = cond.shape[0]
    tn = 1536
    out = pl.pallas_call(
        _mod_kernel,
        grid=(DEPTH, 6 * D_MODEL // tn),
        in_specs=[
            pl.BlockSpec((r, D_MODEL), lambda l, j: (0, 0)),
            pl.BlockSpec((1, D_MODEL, tn), lambda l, j: (l, 0, j)),
            pl.BlockSpec((1, 1, tn), lambda l, j: (l, 0, j)),
        ],
        out_specs=pl.BlockSpec((1, r, tn), lambda l, j: (l, 0, j)),
        out_shape=jax.ShapeDtypeStruct((DEPTH, r, 6 * D_MODEL), F32),
        compiler_params=_cparams("parallel", "parallel"),
        name="modulation",
    )(cond, w_ada, b_ada.reshape(DEPTH, 1, 6 * D_MODEL))
    return out.reshape(DEPTH, r, 6, D_MODEL)


def _mod_spec(tiles_per_row):
    return pl.BlockSpec((1, 6, D_MODEL), lambda i: (i // tiles_per_row, 0, 0))


def _proj_even_kernel(x_ref, g_ref, mod_ref, w_ref, *out_refs, mults):
    m = mod_ref[0]
    h = _modulate(x_ref[...], g_ref[...], m[0:1], m[1:2]).astype(BF16)
    width = out_refs[0].shape[-1]
    for i, (o_ref, mult) in enumerate(zip(out_refs, mults)):
        y = _dot(h, w_ref[:, i * width:(i + 1) * width])
        if mult != 1.0:
            y = y * mult
        o_ref[...] = y.astype(o_ref.dtype)


def _proj_even(x, g, mod, w, tokens_per_row, kv_dtype):
    t = x.shape[0]
    tm = min(512, t)
    width = H_RET * HEAD_DIM
    dtypes = [BF16, BF16, BF16, F32, BF16, kv_dtype, kv_dtype]
    mults = (1.0, HEAD_DIM ** -0.5, 1.0, 1.0, 1.0, 1.0, 1.0)
    return pl.pallas_call(
        functools.partial(_proj_even_kernel, mults=mults),
        grid=(t // tm,),
        in_specs=[
            pl.BlockSpec((tm, D_MODEL), lambda i: (i, 0)),
            pl.BlockSpec((1, D_MODEL), lambda i: (0, 0)),
            _mod_spec(tokens_per_row // tm),
            pl.BlockSpec(w.shape, lambda i: (0, 0)),
        ],
        out_specs=[pl.BlockSpec((tm, width), lambda i: (i, 0)) for _ in dtypes],
        out_shape=[jax.ShapeDtypeStruct((t, width), dt) for dt in dtypes],
        compiler_params=_cparams("parallel"),
        name="proj_even",
    )(x, g, mod, w)


def _retention_kernel(lgf_ref, lgb_ref, q_ref, k_ref, v_ref, g_ref, *rest, n_chunks, has_state, emit_state):
    if has_state:
        sf0_ref, sb0_ref = rest[0], rest[1]
        rest = rest[2:]
    o_ref = rest[0]
    rest = rest[1:]
    if emit_state:
        sfo_ref, sbo_ref = rest[0], rest[1]
        rest = rest[2:]
    acc_ref, kv_ref, st_ref = rest

    c = RET_CHUNK
    d = HEAD_DIM
    pair = pl.program_id(1)
    row = lax.broadcasted_iota(jnp.int32, (c, c), 0).astype(F32)
    col = lax.broadcasted_iota(jnp.int32, (c, c), 1).astype(F32)
    diff = row - col
    pos = lax.broadcasted_iota(jnp.int32, (c, HEAD_PAIR), 0).astype(F32)
    first_half = lax.broadcasted_iota(jnp.int32, (c, HEAD_PAIR), 1) < d

    masks, kdecs, qdecs, cdecs = [], [], [], []
    for j in range(2):
        lgf = lgf_ref[2 * pair + j]
        lgb = lgb_ref[2 * pair + j]
        masks.append(jnp.where(diff >= 0, jnp.exp(lgf * jnp.maximum(diff, 0.0)), 0.0)
                     + jnp.where(diff <= 0, jnp.exp(lgb * jnp.maximum(-diff, 0.0)), 0.0))
        kdecs.append(jnp.where(first_half, jnp.exp(lgf * (c - 1 - pos)), jnp.exp(lgb * pos)))
        qdecs.append(jnp.where(first_half, jnp.exp(lgf * (pos + 1.0)), jnp.exp(lgb * (c - pos))))
        cdecs.append((jnp.exp(jnp.full((1, 1), lgf * c, F32)), jnp.exp(jnp.full((1, 1), lgb * c, F32))))

    def doubled(tile):
        t = tile.astype(F32)
        r = pltpu.roll(t, d, 1)
        return jnp.where(first_half, t, r), jnp.where(first_half, r, t)

    def chunk_rows(n):
        return pl.ds(pl.multiple_of(n * c, c), c)

    def intra(n, carry):
        rows = chunk_rows(n)
        q, k, v = q_ref[0, rows, :], k_ref[0, rows, :], v_ref[0, rows, :]
        k2 = doubled(k)
        for j in range(2):
            lanes = slice(j * d, (j + 1) * d)
            s = (_dot_nt(q[:, lanes], k[:, lanes]) * masks[j]).astype(BF16)
            acc_ref[rows, lanes] = _dot(s, v[:, lanes])
            kv_ref[n, j] = _dot_tn((k2[j] * kdecs[j]).astype(BF16), v[:, lanes])
        return carry

    lax.fori_loop(0, n_chunks, intra, 0, unroll=2)

    for j in range(2):
        if has_state:
            sf0, sb0 = sf0_ref[0, 0, j], sb0_ref[0, 0, j]
        else:
            sf0 = sb0 = jnp.zeros((d, d), F32)

        def scan_f(n, state):
            st_ref[n, j, 0:d, :] = state.astype(BF16)
            return cdecs[j][0] * state + kv_ref[n, j, 0:d, :]

        def scan_b(i, state):
            n = n_chunks - 1 - i
            st_ref[n, j, d:2 * d, :] = state.astype(BF16)
            return cdecs[j][1] * state + kv_ref[n, j, d:2 * d, :]

        fin_f = lax.fori_loop(0, n_chunks, scan_f, sf0)
        fin_b = lax.fori_loop(0, n_chunks, scan_b, sb0)
        if emit_state:
            sfo_ref[0, j] = fin_f
            sbo_ref[0, j] = fin_b

    def finish(n, carry):
        rows = chunk_rows(n)
        q2 = doubled(q_ref[0, rows, :])
        parts = []
        for j in range(2):
            lanes = slice(j * d, (j + 1) * d)
            oj = acc_ref[rows, lanes] + _dot((q2[j] * qdecs[j]).astype(BF16), st_ref[n, j])
            mu = jnp.mean(oj, axis=-1, keepdims=True)
            var = jnp.mean(jnp.square(oj - mu), axis=-1, keepdims=True)
            parts.append((oj - mu) * lax.rsqrt(var + EPS))
        gate = g_ref[0, rows, :]
        on = jnp.concatenate(parts, axis=-1)
        o_ref[0, rows, :] = (on * (gate / (1.0 + jnp.exp(-gate)))).astype(o_ref.dtype)
        return carry

    lax.fori_loop(0, n_chunks, finish, 0, unroll=2)


def _retention(q, k, v, g, lg_f, lg_b, state_f, state_b, ia, emit_state):
    b, l, _ = q.shape
    has_state = state_f is not None
    blk = lambda: pl.BlockSpec((1, l, HEAD_PAIR), lambda bi, p: (bi, 0, p))
    smem = pl.BlockSpec(memory_space=pltpu.SMEM)
    in_specs = [smem, smem, blk(), blk(), blk(), blk()]
    args = [lg_f, lg_b, q, k, v, g]
    if has_state:
        st = lambda: pl.BlockSpec((1, 1, 2, HEAD_DIM, HEAD_DIM), lambda bi, p: (bi, ia, p, 0, 0))
        in_specs += [st(), st()]
        args += [state_f, state_b]
    out_specs = [blk()]
    out_shape = [jax.ShapeDtypeStruct((b, l, H_RET * HEAD_DIM), BF16)]
    if emit_state:
        so = lambda: pl.BlockSpec((1, 2, HEAD_DIM, HEAD_DIM), lambda bi, p: (bi, p, 0, 0))
        out_specs += [so(), so()]
        out_shape += [jax.ShapeDtypeStruct((b, H_RET, HEAD_DIM, HEAD_DIM), F32)] * 2
    return pl.pallas_call(
        functools.partial(_retention_kernel, n_chunks=l // RET_CHUNK, has_state=has_state,
                          emit_state=emit_state),
        grid=(b, H_RET // 2),
        in_specs=in_specs,
        out_specs=out_specs,
        out_shape=out_shape,
        scratch_shapes=[pltpu.VMEM((l, HEAD_PAIR), F32),
                        pltpu.VMEM((l // RET_CHUNK, 2, HEAD_PAIR, HEAD_DIM), F32),
                        pltpu.VMEM((l // RET_CHUNK, 2, HEAD_PAIR, HEAD_DIM), BF16)],
        compiler_params=_cparams("parallel", "parallel"),
        name="retention",
    )(*args)


def _attn_kernel(*refs, nseg, dq, scale, heads, emit_kv):
    q_refs = refs[0:nseg]
    k_refs = refs[nseg:2 * nseg]
    v_refs = refs[2 * nseg:3 * nseg]
    o_ref = refs[3 * nseg]
    for pair in range(heads // 2):
        outs = []
        for j in range(2 * pair, 2 * pair + 2):
            ql = slice(j * dq, (j + 1) * dq)
            vl = slice(j * HEAD_DIM, (j + 1) * HEAD_DIM)
            scores = [_dot_nt(q_refs[s][0, :, ql], k_refs[s][0, :, ql].astype(BF16)) for s in range(nseg)]
            m = functools.reduce(jnp.maximum, [jnp.max(s, axis=-1, keepdims=True) for s in scores])
            acc = None
            den = None
            for s in range(nseg):
                p = jnp.exp((scores[s] - m) * scale)
                ps = jnp.sum(p, axis=-1, keepdims=True)
                pv = _dot(p.astype(BF16), v_refs[s][0, :, vl].astype(BF16))
                acc = pv if acc is None else acc + pv
                den = ps if den is None else den + ps
            outs.append(acc / den)
        o_ref[0, :, pair * HEAD_PAIR:(pair + 1) * HEAD_PAIR] = jnp.concatenate(outs, axis=-1).astype(o_ref.dtype)
    if emit_kv:
        ko_ref, vo_ref = refs[3 * nseg + 1], refs[3 * nseg + 2]
        for j in range(heads):
            ko_ref[0, j] = k_refs[0][0, :, j * dq:(j + 1) * dq]
            vo_ref[0, j] = v_refs[0][0, :, j * HEAD_DIM:(j + 1) * HEAD_DIM]


def _attention(qs, ks, vs, n_heads, dq, scale, tq, heads_per_step=2, emit_kv=False):
    nseg = len(qs)
    b, lq, _ = qs[0].shape
    hs = heads_per_step
    in_specs = [pl.BlockSpec((1, tq, hs * dq), lambda bi, p, qi: (bi, qi, p)) for _ in qs]
    in_specs += [pl.BlockSpec((1, k.shape[1], hs * dq), lambda bi, p, qi: (bi, 0, p)) for k in ks]
    in_specs += [pl.BlockSpec((1, v.shape[1], hs * HEAD_DIM), lambda bi, p, qi: (bi, 0, p)) for v in vs]
    out_specs = [pl.BlockSpec((1, tq, hs * HEAD_DIM), lambda bi, p, qi: (bi, qi, p))]
    out_shape = [jax.ShapeDtypeStruct((b, lq, n_heads * HEAD_DIM), BF16)]
    if emit_kv:
        assert nseg == 1 and tq == lq and dq == HEAD_DIM
        lk = ks[0].shape[1]
        kv = lambda: pl.BlockSpec((1, hs, lk, HEAD_DIM), lambda bi, p, qi: (bi, p, 0, 0))
        out_specs += [kv(), kv()]
        out_shape += [jax.ShapeDtypeStruct((b, n_heads, lk, HEAD_DIM), F32)] * 2
    res = pl.pallas_call(
        functools.partial(_attn_kernel, nseg=nseg, dq=dq, scale=scale, heads=hs, emit_kv=emit_kv),
        grid=(b, n_heads // hs, lq // tq),
        in_specs=in_specs,
        out_specs=out_specs,
        out_shape=out_shape,
        compiler_params=_cparams("parallel", "parallel", "arbitrary"),
        name="attention",
    )(*qs, *ks, *vs)
    return res if emit_kv else res[0]


NA_QROWS = 4
NA_KROWS = NA_QROWS + WIN_R - 1


def _na_block_start(g, rows):
    return np.clip(NA_QROWS * g - WIN_R // 2, 0, rows - NA_KROWS)


def _na_bias_tables(rpb, rows):
    h = rpb.shape[0]
    w = GRID_W
    nd = 2 * WIN_R - 1
    period = 2 * w - 1
    lo = w - WIN_C
    u = jnp.pad(rpb.astype(F32), ((0, 0), (0, 0), (lo, period - lo - (2 * WIN_C - 1))))
    big = jnp.tile(u, (1, 1, w + 1))[..., :w * 2 * w].reshape(h, nd, w, 2 * w)
    colb = big[:, :, ::-1, :w]
    col = np.arange(w)
    win_start = np.clip(col - WIN_C // 2, 0, w - WIN_C)
    valid = (col[None, :] >= win_start[:, None]) & (col[None, :] < win_start[:, None] + WIN_C)
    colb = jnp.where(valid[None, None], colb, NEG_INF)
    neg = jnp.full((h, w, w), NEG_INF, F32)
    nblk = rows // NA_QROWS
    tables = []
    for g in (0, 1, nblk - 1):
        ks = _na_block_start(g, rows)
        blk_rows = []
        for rr in range(NA_QROWS):
            r = NA_QROWS * g + rr
            rs = np.clip(r - WIN_R // 2, 0, rows - WIN_R)
            pieces = []
            for jj in range(NA_KROWS):
                krow = ks + jj
                pieces.append(colb[:, krow - r + WIN_R - 1] if rs <= krow < rs + WIN_R else neg)
            blk_rows.append(jnp.concatenate(pieces, axis=-1))
        tables.append(jnp.concatenate(blk_rows, axis=1))
    return jnp.stack(tables, axis=1)


def _na_latent_kernel(q_ref, k_ref, v_ref, kc_ref, vc_ref, bias_ref, o_ref, *, rows):
    scale = HEAD_DIM ** -0.5
    nq = NA_QROWS * GRID_W
    nk = NA_KROWS * GRID_W
    nblk = rows // NA_QROWS

    def body(g, carry):
        ks = jnp.clip(NA_QROWS * g - WIN_R // 2, 0, rows - NA_KROWS)
        case = jnp.where(g == 0, 0, jnp.where(g == nblk - 1, 2, 1))
        qrows = pl.ds(pl.multiple_of(g * nq, nq), nq)
        krows = pl.ds(pl.multiple_of(ks * GRID_W, GRID_W), nk)
        outs = []
        for j in range(2):
            lanes = slice(j * HEAD_DIM, (j + 1) * HEAD_DIM)
            q = q_ref[0, qrows, lanes]
            s_loc = _dot_nt(q, k_ref[0, krows, lanes]) * scale + bias_ref[j, case]
            s_ctx = _dot_nt(q, kc_ref[0, 0, j].astype(BF16)) * scale
            m = jnp.maximum(jnp.max(s_loc, axis=-1, keepdims=True), jnp.max(s_ctx, axis=-1, keepdims=True))
            p_loc = jnp.exp(s_loc - m)
            p_ctx = jnp.exp(s_ctx - m)
            den = jnp.sum(p_loc, axis=-1, keepdims=True) + jnp.sum(p_ctx, axis=-1, keepdims=True)
            acc = (_dot(p_loc.astype(BF16), v_ref[0, krows, lanes])
                   + _dot(p_ctx.astype(BF16), vc_ref[0, 0, j].astype(BF16)))
            outs.append(acc / den)
        o_ref[0, qrows, :] = jnp.concatenate(outs, axis=-1).astype(o_ref.dtype)
        return carry

    lax.fori_loop(0, nblk, body, 0)


def _na_latent(q, k, v, cache_k, cache_v, ia, bias):
    b, l, _ = q.shape
    lc = cache_k.shape[3]
    rows = l // GRID_W
    assert rows % NA_QROWS == 0 and rows >= NA_KROWS + NA_QROWS
    blk = lambda: pl.BlockSpec((1, l, HEAD_PAIR), lambda p, bi: (bi, 0, p))
    ctx = lambda: pl.BlockSpec((1, 1, 2, lc, HEAD_DIM), lambda p, bi: (bi, ia, p, 0, 0))
    return pl.pallas_call(
        functools.partial(_na_latent_kernel, rows=rows),
        grid=(H_NA // 2, b),
        in_specs=[blk(), blk(), blk(), ctx(), ctx(),
                  pl.BlockSpec((2,) + bias.shape[1:], lambda p, bi: (p, 0, 0, 0))],
        out_specs=blk(),
        out_shape=jax.ShapeDtypeStruct((b, l, H_NA * HEAD_DIM), BF16),
        compiler_params=_cparams("parallel", "parallel"),
        name="na_latent",
    )(q, k, v, cache_k, cache_v, bias)


def _outproj_kernel(*refs, n_in, gate_i):
    a_refs = refs[:n_in]
    w_ref, x_ref, g_ref, mod_ref, o_ref = refs[n_in:]
    y = None
    k0 = 0
    for a_ref in a_refs:
        kw = a_ref.shape[-1]
        part = _dot(a_ref[...], w_ref[k0:k0 + kw, :])
        y = part if y is None else y + part
        k0 += kw
    m = mod_ref[0]
    o_ref[...] = x_ref[...] + m[gate_i:gate_i + 1] * _rms(y, g_ref[...])


def _outproj_residual(acts, w, x, g, mod, tokens_per_row, gate_i):
    t = x.shape[0]
    tm = min(512, t)
    in_specs = [pl.BlockSpec((tm, a.shape[1]), lambda i: (i, 0)) for a in acts]
    in_specs += [
        pl.BlockSpec(w.shape, lambda i: (0, 0)),
        pl.BlockSpec((tm, D_MODEL), lambda i: (i, 0)),
        pl.BlockSpec((1, D_MODEL), lambda i: (0, 0)),
        _mod_spec(tokens_per_row // tm),
    ]
    return pl.pallas_call(
        functools.partial(_outproj_kernel, n_in=len(acts), gate_i=gate_i),
        grid=(t // tm,),
        in_specs=in_specs,
        out_specs=pl.BlockSpec((tm, D_MODEL), lambda i: (i, 0)),
        out_shape=jax.ShapeDtypeStruct((t, D_MODEL), F32),
        compiler_params=_cparams("parallel"),
        name="outproj_residual",
    )(*acts, w, x, g, mod)


def _mlp_kernel(x_ref, gin_ref, gout_ref, mod_ref, w1_ref, w2_ref, o_ref, h_ref, acc_ref):
    f = pl.program_id(1)

    @pl.when(f == 0)
    def _():
        m = mod_ref[0]
        h_ref[...] = _modulate(x_ref[...], gin_ref[...], m[3:4], m[4:5]).astype(BF16)

    a = jnp.maximum(_dot(h_ref[...], w1_ref[...]), 0.0)
    part = _dot((a * a).astype(BF16), w2_ref[...])

    @pl.when(f == 0)
    def _():
        acc_ref[...] = part

    @pl.when(f > 0)
    def _():
        acc_ref[...] += part

    @pl.when(f == pl.num_programs(1) - 1)
    def _():
        m = mod_ref[0]
        o_ref[...] = x_ref[...] + m[5:6] * _rms(acc_ref[...], gout_ref[...])


def _mlp_residual(x, g_in, g_out, mod, w1, w2, tokens_per_row):
    t = x.shape[0]
    tm = min(1024, t)
    tf = 1024
    return pl.pallas_call(
        _mlp_kernel,
        grid=(t // tm, D_FF // tf),
        in_specs=[
            pl.BlockSpec((tm, D_MODEL), lambda i, f: (i, 0)),
            pl.BlockSpec((1, D_MODEL), lambda i, f: (0, 0)),
            pl.BlockSpec((1, D_MODEL), lambda i, f: (0, 0)),
            pl.BlockSpec((1, 6, D_MODEL), lambda i, f: (i // (tokens_per_row // tm), 0, 0)),
            pl.BlockSpec((D_MODEL, tf), lambda i, f: (0, f)),
            pl.BlockSpec((tf, D_MODEL), lambda i, f: (f, 0)),
        ],
        out_specs=pl.BlockSpec((tm, D_MODEL), lambda i, f: (i, 0)),
        out_shape=jax.ShapeDtypeStruct((t, D_MODEL), F32),
        scratch_shapes=[pltpu.VMEM((tm, D_MODEL), BF16), pltpu.VMEM((tm, D_MODEL), F32)],
        compiler_params=_cparams("parallel", "arbitrary"),
        name="mlp_residual",
    )(x, g_in, g_out, mod, w1, w2)


def _rope_lanes(y, table, keep):
    lane = lax.broadcasted_iota(jnp.int32, y.shape, 1)
    prod = y * table
    rot = prod + pltpu.roll(prod, LANES - MLA_ROPE, 1)
    return jnp.where((lane >= MLA_NOPE) & (lane < MLA_NOPE + MLA_ROPE), rot, keep)


def _proj_mla_kernel(*refs, rope):
    (x_ref, g_ref, mod_ref, win_ref, qn_ref, kvn_ref, wuq_ref, wuk_ref, wuv_ref) = refs[:9]
    refs = refs[9:]
    if rope:
        tab_ref, refs = refs[0], refs[1:]
        qlat_ref, refs = refs[0], refs[1:]
    qctx_ref, k_ref, v_ref, ckv_ref, kpe_ref = refs

    m = mod_ref[0]
    h = _modulate(x_ref[...], g_ref[...], m[0:1], m[1:2]).astype(BF16)
    y = _dot(h, win_ref[...])
    c_q = y[:, :MLA_Q_LORA]
    c_kv = _rms(y[:, MLA_Q_LORA:MLA_Q_LORA + MLA_KV_LORA], kvn_ref[...])
    pe_blk = y[:, MLA_Q_LORA + MLA_KV_LORA:]
    ckv_ref[...] = c_kv
    kpe_ref[...] = pe_blk[:, MLA_NOPE:MLA_NOPE + MLA_ROPE]

    lane = lax.broadcasted_iota(jnp.int32, pe_blk.shape, 1)
    if rope:
        table = tab_ref[...]
        k_pe = _rope_lanes(pe_blk, table, jnp.zeros_like(pe_blk))
    else:
        k_pe = jnp.where((lane >= MLA_NOPE) & (lane < MLA_NOPE + MLA_ROPE), pe_blk, 0.0)

    q = _dot(_rms(c_q, qn_ref[...]).astype(BF16), wuq_ref[...])
    c_kv_b = c_kv.astype(BF16)
    kn = _dot(c_kv_b, wuk_ref[...])
    v_ref[...] = _dot(c_kv_b, wuv_ref[...]).astype(v_ref.dtype)
    for hd in range(MLA_HEADS):
        lanes = slice(hd * MLA_QK_PAD, (hd + 1) * MLA_QK_PAD)
        qh = q[:, lanes]
        qctx_ref[:, lanes] = qh.astype(qctx_ref.dtype)
        if rope:
            qlat_ref[:, lanes] = _rope_lanes(qh, table, qh).astype(qlat_ref.dtype)
        k_ref[:, lanes] = (kn[:, lanes] + k_pe).astype(k_ref.dtype)


def _proj_mla(x, g, mod, w_in, q_norm, kv_norm, w_uq, w_uk, w_uv, tokens_per_row, rope_table):
    t = x.shape[0]
    tm = min(512, t)
    rope = rope_table is not None
    full = lambda a: pl.BlockSpec(a.shape, lambda i: (0,) * a.ndim)
    tok = lambda n: pl.BlockSpec((tm, n), lambda i: (i, 0))
    in_specs = [tok(D_MODEL), full(g), _mod_spec(tokens_per_row // tm), full(w_in), full(q_norm),
                full(kv_norm), full(w_uq), full(w_uk), full(w_uv)]
    args = [x, g, mod, w_in, q_norm, kv_norm, w_uq, w_uk, w_uv]
    qk = MLA_HEADS * MLA_QK_PAD
    out_specs, out_shape = [], []
    if rope:
        tiles_per_seq = rope_table.shape[0] // tm
        in_specs.append(pl.BlockSpec((tm, LANES), lambda i: (i % tiles_per_seq, 0)))
        args.append(rope_table)
        out_specs.append(tok(qk))
        out_shape.append(jax.ShapeDtypeStruct((t, qk), BF16))
    out_specs += [tok(qk), tok(qk), tok(MLA_HEADS * MLA_V), tok(MLA_KV_LORA), tok(MLA_ROPE)]
    out_shape += [jax.ShapeDtypeStruct((t, qk), BF16), jax.ShapeDtypeStruct((t, qk), BF16),
                  jax.ShapeDtypeStruct((t, MLA_HEADS * MLA_V), BF16),
                  jax.ShapeDtypeStruct((t, MLA_KV_LORA), F32), jax.ShapeDtypeStruct((t, MLA_ROPE), F32)]
    return pl.pallas_call(
        functools.partial(_proj_mla_kernel, rope=rope),
        grid=(t // tm,),
        in_specs=in_specs,
        out_specs=out_specs,
        out_shape=out_shape,
        compiler_params=_cparams("parallel"),
        name="proj_mla",
    )(*args)


def _ctx_kv_kernel(ckv_ref, kpe_ref, wuk_ref, wuv_ref, place_ref, k_ref, v_ref):
    c = ckv_ref[...].astype(BF16)
    kn = _dot(c, wuk_ref[...])
    k_pe = _dot(kpe_ref[...].astype(BF16), place_ref[...])
    v_ref[...] = _dot(c, wuv_ref[...]).astype(v_ref.dtype)
    for hd in range(MLA_HEADS):
        lanes = slice(hd * MLA_QK_PAD, (hd + 1) * MLA_QK_PAD)
        k_ref[:, lanes] = (kn[:, lanes] + k_pe).astype(k_ref.dtype)


def _ctx_kv(ckv, kpe, w_uk, w_uv, place):
    t = ckv.shape[0]
    tm = min(512, t)
    full = lambda a: pl.BlockSpec(a.shape, lambda i: (0,) * a.ndim)
    tok = lambda n: pl.BlockSpec((tm, n), lambda i: (i, 0))
    return pl.pallas_call(
        _ctx_kv_kernel,
        grid=(t // tm,),
        in_specs=[tok(MLA_KV_LORA), tok(MLA_ROPE), full(w_uk), full(w_uv), full(place)],
        out_specs=[tok(MLA_HEADS * MLA_QK_PAD), tok(MLA_HEADS * MLA_V)],
        out_shape=[jax.ShapeDtypeStruct((t, MLA_HEADS * MLA_QK_PAD), BF16),
                   jax.ShapeDtypeStruct((t, MLA_HEADS * MLA_V), BF16)],
        compiler_params=_cparams("parallel"),
        name="mla_ctx_kv",
    )(ckv, kpe, w_uk, w_uv, place)


def _pair_swap(w):
    return w.reshape(w.shape[:-1] + (w.shape[-1] // 2, 2))[..., ::-1].reshape(w.shape)


def _mla_weights(w_in, w_uq, w_uk):
    k_pe_cols = w_in[:, MLA_Q_LORA + MLA_KV_LORA:]
    w_in_ext = jnp.concatenate(
        [w_in[:, :MLA_Q_LORA + MLA_KV_LORA], jnp.zeros((D_MODEL, MLA_NOPE), w_in.dtype),
         k_pe_cols, _pair_swap(k_pe_cols)], axis=1).astype(BF16)
    uq = w_uq.reshape(MLA_Q_LORA, MLA_HEADS, MLA_NOPE + MLA_ROPE)
    uq_ext = jnp.concatenate([uq, _pair_swap(uq[..., MLA_NOPE:])], axis=-1)
    uq_ext = uq_ext.reshape(MLA_Q_LORA, MLA_HEADS * MLA_QK_PAD).astype(BF16)
    uk = w_uk.reshape(MLA_KV_LORA, MLA_HEADS, MLA_NOPE)
    uk_ext = jnp.concatenate([uk, jnp.zeros((MLA_KV_LORA, MLA_HEADS, MLA_QK_PAD - MLA_NOPE), uk.dtype)], axis=-1)
    uk_ext = uk_ext.reshape(MLA_KV_LORA, MLA_HEADS * MLA_QK_PAD).astype(BF16)
    return w_in_ext, uq_ext, uk_ext


def _rope_table(l):
    t = jnp.arange(l)
    row = (t // GRID_W).astype(F32)
    col = (t % GRID_W).astype(F32)
    nf = MLA_ROPE // 4
    inv = ROPE_BASE ** (-jnp.arange(nf, dtype=F32) / nf)
    ang = jnp.concatenate([row[:, None] * inv, col[:, None] * inv], axis=-1)
    cos, sin = jnp.cos(ang), jnp.sin(ang)
    cc = jnp.repeat(cos, 2, axis=-1)
    ss = jnp.stack([-sin, sin], axis=-1).reshape(l, MLA_ROPE)
    return jnp.concatenate([jnp.ones((l, MLA_NOPE), F32), cc, ss], axis=-1)


def kernel(x_prompt, x_sample, state_ret_fwd, state_ret_bwd, cache_na_k, cache_na_v, cache_mla_ckv,
           cache_mla_kpe, c, c_ctx, w_ada, b_ada, norm_gains, w_mlp_in, w_mlp_out, w_in_ac, w_out_ac,
           ret_decay_fwd, ret_decay_bwd, na_rpb, w_in_c, mla_q_norm, mla_kv_norm, w_uq, w_uk, w_uv, w_out_c):
    bp, lp, _ = x_prompt.shape
    bs, ls, _ = x_sample.shape
    lc = cache_mla_ckv.shape[2]
    xp = x_prompt.reshape(bp * lp, D_MODEL)
    xs = x_sample.reshape(bs * ls, D_MODEL)

    n_cond = bs + 1
    cond = jnp.concatenate([c, c_ctx[None]], axis=0)
    mods = _modulation_all(cond, w_ada, b_ada)
    rope_table = _rope_table(ls)
    place = jnp.zeros((MLA_ROPE, MLA_QK_PAD), BF16).at[
        jnp.arange(MLA_ROPE), MLA_NOPE + jnp.arange(MLA_ROPE)].set(1.0)

    ret_f, ret_b, na_k, na_v, mla_ckv, mla_kpe = [], [], [], [], [], []
    for layer in range(DEPTH):
        mod_s = mods[layer, :bs]
        mod_p = mods[layer, bs:n_cond]
        g = norm_gains[layer]
        g0, g1, g2, g3 = (g[i:i + 1] for i in range(4))
        tp, ts = bp * lp, ls
        if layer % 2 == 0:
            ia = layer // 2
            w_in = w_in_ac[ia].astype(BF16)
            w_out = w_out_ac[ia].astype(BF16)
            lg_f = jnp.log1p(-jnp.exp2(ret_decay_fwd[ia].astype(F32)))
            lg_b = jnp.log1p(-jnp.exp2(ret_decay_bwd[ia].astype(F32)))
            seq = lambda a, b, l: a.reshape(b, l, a.shape[-1])

            rq, rk, rv, rg, nq, nk, nv = [seq(a, bp, lp) for a in _proj_even(xp, g0, mod_p, w_in, tp, F32)]
            o_ret, s_f, s_b = _retention(rq, rk, rv, rg, lg_f, lg_b, None, None, ia, True)
            o_na, k_out, v_out = _attention([nq], [nk], [nv], H_NA, HEAD_DIM, HEAD_DIM ** -0.5, lp,
                                             heads_per_step=H_NA, emit_kv=True)
            ret_f.append(s_f)
            ret_b.append(s_b)
            na_k.append(k_out)
            na_v.append(v_out)
            xp = _outproj_residual([o_ret.reshape(tp, -1), o_na.reshape(tp, -1)], w_out, xp, g1, mod_p, tp, 2)

            rq, rk, rv, rg, nq, nk, nv = [seq(a, bs, ls) for a in _proj_even(xs, g0, mod_s, w_in, ts, BF16)]
            o_ret = _retention(rq, rk, rv, rg, lg_f, lg_b, state_ret_fwd, state_ret_bwd, ia, False)[0]
            o_na = _na_latent(nq, nk, nv, cache_na_k, cache_na_v, ia, _na_bias_tables(na_rpb[ia], ls // GRID_W))
            xs = _outproj_residual([o_ret.reshape(bs * ls, -1), o_na.reshape(bs * ls, -1)], w_out, xs, g1,
                                   mod_s, ts, 2)
        else:
            ic = layer // 2
            w_in_ext, uq_ext, uk_ext = _mla_weights(w_in_c[ic], w_uq[ic], w_uk[ic])
            uv = w_uv[ic].astype(BF16)
            w_out = w_out_c[ic].astype(BF16)
            qn, kvn = mla_q_norm[ic][None], mla_kv_norm[ic][None]

            q, k, v, ckv, kpe = _proj_mla(xp, g0, mod_p, w_in_ext, qn, kvn, uq_ext, uk_ext, uv, tp, None)
            mla_ckv.append(ckv.reshape(bp, lp, MLA_KV_LORA))
            mla_kpe.append(kpe.reshape(bp, lp, MLA_ROPE))
            sq = lambda a, b, l: a.reshape(b, l, a.shape[-1])
            o = _attention([sq(q, bp, lp)], [sq(k, bp, lp)], [sq(v, bp, lp)], MLA_HEADS, MLA_QK_PAD,
                           MLA_SCALE, lp, heads_per_step=8)
            xp = _outproj_residual([o.reshape(tp, -1)], w_out, xp, g1, mod_p, tp, 2)

            q_lat, q_ctx, k, v, _, _ = _proj_mla(xs, g0, mod_s, w_in_ext, qn, kvn, uq_ext, uk_ext, uv, ts,
                                                 rope_table)
            k_c, v_c = _ctx_kv(cache_mla_ckv[:, ic].reshape(bs * lc, MLA_KV_LORA),
                               cache_mla_kpe[:, ic].reshape(bs * lc, MLA_ROPE), uk_ext, uv, place)
            o = _attention([sq(q_lat, bs, ls), sq(q_ctx, bs, ls)], [sq(k, bs, ls), sq(k_c, bs, lc)],
                           [sq(v, bs, ls), sq(v_c, bs, lc)], MLA_HEADS, MLA_QK_PAD, MLA_SCALE, 256)
            xs = _outproj_residual([o.reshape(bs * ls, -1)], w_out, xs, g1, mod_s, ts, 2)

        w1 = w_mlp_in[layer].astype(BF16)
        w2 = w_mlp_out[layer].astype(BF16)
        xp = _mlp_residual(xp, g2, g3, mod_p, w1, w2, tp)
        xs = _mlp_residual(xs, g2, g3, mod_s, w1, w2, ts)

    return (xp.reshape(bp, lp, D_MODEL), xs.reshape(bs, ls, D_MODEL),
            jnp.stack(ret_f, axis=1), jnp.stack(ret_b, axis=1),
            jnp.stack(na_k, axis=1), jnp.stack(na_v, axis=1),
            jnp.stack(mla_ckv, axis=1), jnp.stack(mla_kpe, axis=1))
```

```python
import functools

import numpy as np
import jax
import jax.numpy as jnp
from jax import lax
from jax.experimental import pallas as pl
from jax.experimental.pallas import tpu as pltpu

D_MODEL = 1024
DEPTH = 4
GRID_W = 64
HEAD_DIM = 64
N_HEADS = D_MODEL // HEAD_DIM
H_RET = N_HEADS // 2
H_NA = N_HEADS - H_RET
RET_CHUNK = 128
WIN_R = 8
WIN_C = 16
MLA_HEADS = N_HEADS
MLA_Q_LORA = 384
MLA_KV_LORA = 256
MLA_NOPE = 64
MLA_ROPE = 32
MLA_V = 64
MLA_SCALE = (MLA_NOPE + MLA_ROPE) ** -0.5
ROPE_BASE = 10000.0
D_FF = 4 * D_MODEL
EPS = 1e-6
NEG_INF = -1e30

LANES = 128
HEAD_PAIR = 2 * HEAD_DIM
MLA_QK_PAD = 128
VMEM_LIMIT = 56 * 1024 * 1024

F32 = jnp.float32
BF16 = jnp.bfloat16


def _cparams(*sem):
    return pltpu.CompilerParams(dimension_semantics=sem, vmem_limit_bytes=VMEM_LIMIT)


def _dot(a, b):
    return jnp.dot(a, b, preferred_element_type=F32)


def _dot_nt(a, b):
    return lax.dot_general(a, b, (((1,), (1,)), ((), ())), preferred_element_type=F32)


def _dot_tn(a, b):
    return lax.dot_general(a, b, (((0,), (0,)), ((), ())), preferred_element_type=F32)


def _rms(x, g):
    ms = jnp.mean(x * x, axis=-1, keepdims=True)
    return x * lax.rsqrt(ms + EPS) * g


def _modulate(x, g, shift, scale):
    return _rms(x, g) * (1.0 + scale) + shift


def _mod_kernel(c_ref, w_ref, b_ref, o_ref):
    c = c_ref[...]
    s = c / (1.0 + jnp.exp(-c))
    o_ref[0] = _dot(s.astype(BF16), w_ref[0].astype(BF16)) + b_ref[0]


def _modulation_all(cond, w_ada, b_ada):
    r = cond.shape[0]
    tn = 1536
    out = pl.pallas_call(
        _mod_kernel,
        grid=(DEPTH, 6 * D_MODEL // tn),
        in_specs=[
            pl.BlockSpec((r, D_MODEL), lambda l, j: (0, 0)),
            pl.BlockSpec((1, D_MODEL, tn), lambda l, j: (l, 0, j)),
            pl.BlockSpec((1, 1, tn), lambda l, j: (l, 0, j)),
        ],
        out_specs=pl.BlockSpec((1, r, tn), lambda l, j: (l, 0, j)),
        out_shape=jax.ShapeDtypeStruct((DEPTH, r, 6 * D_MODEL), F32),
        compiler_params=_cparams("parallel", "parallel"),
        name="modulation",
    )(cond, w_ada, b_ada.reshape(DEPTH, 1, 6 * D_MODEL))
    return out.reshape(DEPTH, r, 6, D_MODEL)


def _mod_spec(tiles_per_row):
    return pl.BlockSpec((1, 6, D_MODEL), lambda i: (i // tiles_per_row, 0, 0))


def _proj_even_kernel(x_ref, g_ref, mod_ref, w_ref, *out_refs, mults):
    m = mod_ref[0]
    h = _modulate(x_ref[...], g_ref[...], m[0:1], m[1:2]).astype(BF16)
    width = out_refs[0].shape[-1]
    for i, (o_ref, mult) in enumerate(zip(out_refs, mults)):
        y = _dot(h, w_ref[:, i * width:(i + 1) * width])
        if mult != 1.0:
            y = y * mult
        o_ref[...] = y.astype(o_ref.dtype)


def _proj_even(x, g, mod, w, tokens_per_row, kv_dtype):
    t = x.shape[0]
    tm = min(512, t)
    width = H_RET * HEAD_DIM
    dtypes = [BF16, BF16, BF16, F32, BF16, kv_dtype, kv_dtype]
    mults = (1.0, HEAD_DIM ** -0.5, 1.0, 1.0, 1.0, 1.0, 1.0)
    return pl.pallas_call(
        functools.partial(_proj_even_kernel, mults=mults),
        grid=(t // tm,),
        in_specs=[
            pl.BlockSpec((tm, D_MODEL), lambda i: (i, 0)),
            pl.BlockSpec((1, D_MODEL), lambda i: (0, 0)),
            _mod_spec(tokens_per_row // tm),
            pl.BlockSpec(w.shape, lambda i: (0, 0)),
        ],
        out_specs=[pl.BlockSpec((tm, width), lambda i: (i, 0)) for _ in dtypes],
        out_shape=[jax.ShapeDtypeStruct((t, width), dt) for dt in dtypes],
        compiler_params=_cparams("parallel"),
        name="proj_even",
    )(x, g, mod, w)


def _retention_kernel(lgf_ref, lgb_ref, q_ref, k_ref, v_ref, g_ref, *rest, n_chunks, has_state, emit_state):
    if has_state:
        sf0_ref, sb0_ref = rest[0], rest[1]
        rest = rest[2:]
    o_ref = rest[0]
    rest = rest[1:]
    if emit_state:
        sfo_ref, sbo_ref = rest[0], rest[1]
        rest = rest[2:]
    acc_ref, kv_ref, st_ref = rest

    c = RET_CHUNK
    d = HEAD_DIM
    pair = pl.program_id(1)
    row = lax.broadcasted_iota(jnp.int32, (c, c), 0).astype(F32)
    col = lax.broadcasted_iota(jnp.int32, (c, c), 1).astype(F32)
    diff = row - col
    pos = lax.broadcasted_iota(jnp.int32, (c, HEAD_PAIR), 0).astype(F32)
    first_half = lax.broadcasted_iota(jnp.int32, (c, HEAD_PAIR), 1) < d

    masks, kdecs, qdecs, cdecs = [], [], [], []
    for j in range(2):
        lgf = lgf_ref[2 * pair + j]
        lgb = lgb_ref[2 * pair + j]
        masks.append(jnp.where(diff >= 0, jnp.exp(lgf * jnp.maximum(diff, 0.0)), 0.0)
                     + jnp.where(diff <= 0, jnp.exp(lgb * jnp.maximum(-diff, 0.0)), 0.0))
        kdecs.append(jnp.where(first_half, jnp.exp(lgf * (c - 1 - pos)), jnp.exp(lgb * pos)))
        qdecs.append(jnp.where(first_half, jnp.exp(lgf * (pos + 1.0)), jnp.exp(lgb * (c - pos))))
        cdecs.append((jnp.exp(jnp.full((1, 1), lgf * c, F32)), jnp.exp(jnp.full((1, 1), lgb * c, F32))))

    def doubled(tile):
        t = tile.astype(F32)
        r = pltpu.roll(t, d, 1)
        return jnp.where(first_half, t, r), jnp.where(first_half, r, t)

    def chunk_rows(n):
        return pl.ds(pl.multiple_of(n * c, c), c)

    halves = (first_half, jnp.logical_not(first_half))
    same_head = ((lax.broadcasted_iota(jnp.int32, (HEAD_PAIR, HEAD_PAIR), 0) < d)
                 == (lax.broadcasted_iota(jnp.int32, (HEAD_PAIR, HEAD_PAIR), 1) < d))
    head_mean = jnp.where(same_head, 1.0 / d, 0.0).astype(BF16)

    def group_mean(x):
        hi = x.astype(BF16)
        lo = (x - hi.astype(F32)).astype(BF16)
        return _dot(hi, head_mean) + _dot(lo, head_mean)

    def intra(n, carry):
        rows = chunk_rows(n)
        q, k, v = q_ref[0, rows, :], k_ref[0, rows, :], v_ref[0, rows, :]
        kf = k.astype(F32)
        k2 = doubled(k)
        o = None
        for j in range(2):
            kj = jnp.where(halves[j], kf, 0.0).astype(BF16)
            s = (_dot_nt(q, kj) * masks[j]).astype(BF16)
            oj = _dot(s, v)
            o = oj if o is None else jnp.where(first_half, o, oj)
            kv = _dot_tn((k2[j] * kdecs[j]).astype(BF16), v)
            kv_ref[n, j] = jnp.where(halves[j], kv, 0.0)
        acc_ref[rows, :] = o
        return carry

    lax.fori_loop(0, n_chunks, intra, 0, unroll=min(4, n_chunks))

    for j in range(2):
        zeros = jnp.zeros((d, d), F32)
        if has_state:
            place = (lambda s: jnp.concatenate([s, zeros], axis=1)) if j == 0 else (
                lambda s: jnp.concatenate([zeros, s], axis=1))
            sf0, sb0 = place(sf0_ref[0, 0, j]), place(sb0_ref[0, 0, j])
        else:
            sf0 = sb0 = jnp.zeros((d, HEAD_PAIR), F32)

        def scan_f(n, state):
            st_ref[n, 2 * j * d:(2 * j + 1) * d, :] = state.astype(BF16)
            return cdecs[j][0] * state + kv_ref[n, j, 0:d, :]

        def scan_b(i, state):
            n = n_chunks - 1 - i
            st_ref[n, (2 * j + 1) * d:(2 * j + 2) * d, :] = state.astype(BF16)
            return cdecs[j][1] * state + kv_ref[n, j, d:2 * d, :]

        fin_f = lax.fori_loop(0, n_chunks, scan_f, sf0)
        fin_b = lax.fori_loop(0, n_chunks, scan_b, sb0)
        if emit_state:
            sfo_ref[0, j] = fin_f[:, j * d:(j + 1) * d]
            sbo_ref[0, j] = fin_b[:, j * d:(j + 1) * d]

    def finish(n, carry):
        rows = chunk_rows(n)
        q2 = doubled(q_ref[0, rows, :])
        qd = jnp.concatenate([(q2[j] * qdecs[j]).astype(BF16) for j in range(2)], axis=1)
        o = acc_ref[rows, :] + _dot(qd, st_ref[n])
        cen = o - group_mean(o)
        on = cen * lax.rsqrt(group_mean(cen * cen) + EPS)
        gate = g_ref[0, rows, :]
        o_ref[0, rows, :] = (on * (gate / (1.0 + jnp.exp(-gate)))).astype(o_ref.dtype)
        return carry

    lax.fori_loop(0, n_chunks, finish, 0, unroll=min(4, n_chunks))


def _retention(q, k, v, g, lg_f, lg_b, state_f, state_b, ia, emit_state):
    b, l, _ = q.shape
    has_state = state_f is not None
    blk = lambda: pl.BlockSpec((1, l, HEAD_PAIR), lambda bi, p: (bi, 0, p))
    smem = pl.BlockSpec(memory_space=pltpu.SMEM)
    in_specs = [smem, smem, blk(), blk(), blk(), blk()]
    args = [lg_f, lg_b, q, k, v, g]
    if has_state:
        st = lambda: pl.BlockSpec((1, 1, 2, HEAD_DIM, HEAD_DIM), lambda bi, p: (bi, ia, p, 0, 0))
        in_specs += [st(), st()]
        args += [state_f, state_b]
    out_specs = [blk()]
    out_shape = [jax.ShapeDtypeStruct((b, l, H_RET * HEAD_DIM), BF16)]
    if emit_state:
        so = lambda: pl.BlockSpec((1, 2, HEAD_DIM, HEAD_DIM), lambda bi, p: (bi, p, 0, 0))
        out_specs += [so(), so()]
        out_shape += [jax.ShapeDtypeStruct((b, H_RET, HEAD_DIM, HEAD_DIM), F32)] * 2
    return pl.pallas_call(
        functools.partial(_retention_kernel, n_chunks=l // RET_CHUNK, has_state=has_state,
                          emit_state=emit_state),
        grid=(b, H_RET // 2),
        in_specs=in_specs,
        out_specs=out_specs,
        out_shape=out_shape,
        scratch_shapes=[pltpu.VMEM((l, HEAD_PAIR), F32),
                        pltpu.VMEM((l // RET_CHUNK, 2, HEAD_PAIR, HEAD_PAIR), F32),
                        pltpu.VMEM((l // RET_CHUNK, 2 * HEAD_PAIR, HEAD_PAIR), BF16)],
        compiler_params=_cparams("parallel", "parallel"),
        name="retention",
    )(*args)


LOG2_E = 1.4426950408889634


def _attn_aug_kernel(*refs, nseg, dq, scale, heads):
    q_refs = refs[0:nseg]
    k_refs = refs[nseg:2 * nseg]
    v_refs = refs[2 * nseg:3 * nseg]
    o_ref = refs[3 * nseg]
    first_half = lax.broadcasted_iota(jnp.int32, (o_ref.shape[1], HEAD_PAIR), 1) < HEAD_DIM
    for pair in range(heads // 2):
        res = []
        for j in range(2 * pair, 2 * pair + 2):
            ql = slice(j * dq, (j + 1) * dq)
            vl = slice(j * HEAD_PAIR, (j + 1) * HEAD_PAIR)
            scores = [_dot_nt(q_refs[s][0, :, ql], k_refs[s][0, :, ql]) for s in range(nseg)]
            m = functools.reduce(jnp.maximum, [jnp.max(s, axis=-1, keepdims=True) for s in scores])
            acc = None
            for s in range(nseg):
                p = jnp.exp2((scores[s] - m) * (scale * LOG2_E)).astype(BF16)
                pv = _dot(p, v_refs[s][0, :, vl])
                acc = pv if acc is None else acc + pv
            res.append(acc / pltpu.roll(acc, HEAD_DIM, 1))
        out = jnp.where(first_half, res[0], pltpu.roll(res[1], HEAD_DIM, 1))
        o_ref[0, :, pair * HEAD_PAIR:(pair + 1) * HEAD_PAIR] = out.astype(o_ref.dtype)


def _attention_aug(qs, ks, vs, n_heads, dq, scale, tq, heads_per_step=2):
    nseg = len(qs)
    b, lq, _ = qs[0].shape
    hs = heads_per_step
    in_specs = [pl.BlockSpec((1, tq, hs * dq), lambda bi, p, qi: (bi, qi, p)) for _ in qs]
    in_specs += [pl.BlockSpec((1, k.shape[1], hs * dq), lambda bi, p, qi: (bi, 0, p)) for k in ks]
    in_specs += [pl.BlockSpec((1, v.shape[1], hs * HEAD_PAIR), lambda bi, p, qi: (bi, 0, p)) for v in vs]
    return pl.pallas_call(
        functools.partial(_attn_aug_kernel, nseg=nseg, dq=dq, scale=scale, heads=hs),
        grid=(b, n_heads // hs, lq // tq),
        in_specs=in_specs,
        out_specs=pl.BlockSpec((1, tq, hs * HEAD_DIM), lambda bi, p, qi: (bi, qi, p)),
        out_shape=jax.ShapeDtypeStruct((b, lq, n_heads * HEAD_DIM), BF16),
        compiler_params=_cparams("parallel", "parallel", "arbitrary"),
        name="attention_mla",
    )(*qs, *ks, *vs)


def _attn_kernel(*refs, nseg, dq, scale, heads, emit_kv):
    q_refs = refs[0:nseg]
    k_refs = refs[nseg:2 * nseg]
    v_refs = refs[2 * nseg:3 * nseg]
    o_ref = refs[3 * nseg]
    for pair in range(heads // 2):
        outs = []
        for j in range(2 * pair, 2 * pair + 2):
            ql = slice(j * dq, (j + 1) * dq)
            vl = slice(j * HEAD_DIM, (j + 1) * HEAD_DIM)
            scores = [_dot_nt(q_refs[s][0, :, ql], k_refs[s][0, :, ql].astype(BF16)) for s in range(nseg)]
            m = functools.reduce(jnp.maximum, [jnp.max(s, axis=-1, keepdims=True) for s in scores])
            acc = None
            den = None
            for s in range(nseg):
                p = jnp.exp((scores[s] - m) * scale)
                ps = jnp.sum(p, axis=-1, keepdims=True)
                pv = _dot(p.astype(BF16), v_refs[s][0, :, vl].astype(BF16))
                acc = pv if acc is None else acc + pv
                den = ps if den is None else den + ps
            outs.append(acc / den)
        o_ref[0, :, pair * HEAD_PAIR:(pair + 1) * HEAD_PAIR] = jnp.concatenate(outs, axis=-1).astype(o_ref.dtype)
    if emit_kv:
        ko_ref, vo_ref = refs[3 * nseg + 1], refs[3 * nseg + 2]
        for j in range(heads):
            ko_ref[0, j] = k_refs[0][0, :, j * dq:(j + 1) * dq]
            vo_ref[0, j] = v_refs[0][0, :, j * HEAD_DIM:(j + 1) * HEAD_DIM]


def _attention(qs, ks, vs, n_heads, dq, scale, tq, heads_per_step=2, emit_kv=False):
    nseg = len(qs)
    b, lq, _ = qs[0].shape
    hs = heads_per_step
    in_specs = [pl.BlockSpec((1, tq, hs * dq), lambda bi, p, qi: (bi, qi, p)) for _ in qs]
    in_specs += [pl.BlockSpec((1, k.shape[1], hs * dq), lambda bi, p, qi: (bi, 0, p)) for k in ks]
    in_specs += [pl.BlockSpec((1, v.shape[1], hs * HEAD_DIM), lambda bi, p, qi: (bi, 0, p)) for v in vs]
    out_specs = [pl.BlockSpec((1, tq, hs * HEAD_DIM), lambda bi, p, qi: (bi, qi, p))]
    out_shape = [jax.ShapeDtypeStruct((b, lq, n_heads * HEAD_DIM), BF16)]
    if emit_kv:
        assert nseg == 1 and tq == lq and dq == HEAD_DIM
        lk = ks[0].shape[1]
        kv = lambda: pl.BlockSpec((1, hs, lk, HEAD_DIM), lambda bi, p, qi: (bi, p, 0, 0))
        out_specs += [kv(), kv()]
        out_shape += [jax.ShapeDtypeStruct((b, n_heads, lk, HEAD_DIM), F32)] * 2
    res = pl.pallas_call(
        functools.partial(_attn_kernel, nseg=nseg, dq=dq, scale=scale, heads=hs, emit_kv=emit_kv),
        grid=(b, n_heads // hs, lq // tq),
        in_specs=in_specs,
        out_specs=out_specs,
        out_shape=out_shape,
        compiler_params=_cparams("parallel", "parallel", "arbitrary"),
        name="attention",
    )(*qs, *ks, *vs)
    return res if emit_kv else res[0]


NA_QROWS = 4
NA_KROWS = NA_QROWS + WIN_R - 1


def _na_block_start(g, rows):
    return np.clip(NA_QROWS * g - WIN_R // 2, 0, rows - NA_KROWS)


def _na_bias_tables(rpb, rows):
    h = rpb.shape[0]
    w = GRID_W
    nd = 2 * WIN_R - 1
    period = 2 * w - 1
    lo = w - WIN_C
    u = jnp.pad(rpb.astype(F32), ((0, 0), (0, 0), (lo, period - lo - (2 * WIN_C - 1))))
    big = jnp.tile(u, (1, 1, w + 1))[..., :w * 2 * w].reshape(h, nd, w, 2 * w)
    colb = big[:, :, ::-1, :w]
    col = np.arange(w)
    win_start = np.clip(col - WIN_C // 2, 0, w - WIN_C)
    valid = (col[None, :] >= win_start[:, None]) & (col[None, :] < win_start[:, None] + WIN_C)
    colb = jnp.where(valid[None, None], colb, NEG_INF)
    neg = jnp.full((h, w, w), NEG_INF, F32)
    nblk = rows // NA_QROWS
    tables = []
    for g in (0, 1, nblk - 1):
        ks = _na_block_start(g, rows)
        blk_rows = []
        for rr in range(NA_QROWS):
            r = NA_QROWS * g + rr
            rs = np.clip(r - WIN_R // 2, 0, rows - WIN_R)
            pieces = []
            for jj in range(NA_KROWS):
                krow = ks + jj
                pieces.append(colb[:, krow - r + WIN_R - 1] if rs <= krow < rs + WIN_R else neg)
            blk_rows.append(jnp.concatenate(pieces, axis=-1))
        tables.append(jnp.concatenate(blk_rows, axis=1))
    return jnp.stack(tables, axis=1)


def _na_latent_kernel(q_ref, k_ref, v_ref, kc_ref, vc_ref, bias_ref, o_ref, *, rows):
    scale = HEAD_DIM ** -0.5
    nq = NA_QROWS * GRID_W
    nk = NA_KROWS * GRID_W
    nblk = rows // NA_QROWS

    def body(g, carry):
        ks = jnp.clip(NA_QROWS * g - WIN_R // 2, 0, rows - NA_KROWS)
        case = jnp.where(g == 0, 0, jnp.where(g == nblk - 1, 2, 1))
        qrows = pl.ds(pl.multiple_of(g * nq, nq), nq)
        krows = pl.ds(pl.multiple_of(ks * GRID_W, GRID_W), nk)
        outs = []
        for j in range(2):
            lanes = slice(j * HEAD_DIM, (j + 1) * HEAD_DIM)
            q = q_ref[0, qrows, lanes]
            s_loc = _dot_nt(q, k_ref[0, krows, lanes]) * scale + bias_ref[j, case]
            s_ctx = _dot_nt(q, kc_ref[0, 0, j].astype(BF16)) * scale
            m = jnp.maximum(jnp.max(s_loc, axis=-1, keepdims=True), jnp.max(s_ctx, axis=-1, keepdims=True))
            p_loc = jnp.exp(s_loc - m)
            p_ctx = jnp.exp(s_ctx - m)
            den = jnp.sum(p_loc, axis=-1, keepdims=True) + jnp.sum(p_ctx, axis=-1, keepdims=True)
            acc = (_dot(p_loc.astype(BF16), v_ref[0, krows, lanes])
                   + _dot(p_ctx.astype(BF16), vc_ref[0, 0, j].astype(BF16)))
            outs.append(acc / den)
        o_ref[0, qrows, :] = jnp.concatenate(outs, axis=-1).astype(o_ref.dtype)
        return carry

    lax.fori_loop(0, nblk, body, 0)


def _na_latent(q, k, v, cache_k, cache_v, ia, bias):
    b, l, _ = q.shape
    lc = cache_k.shape[3]
    rows = l // GRID_W
    assert rows % NA_QROWS == 0 and rows >= NA_KROWS + NA_QROWS
    blk = lambda: pl.BlockSpec((1, l, HEAD_PAIR), lambda p, bi: (bi, 0, p))
    ctx = lambda: pl.BlockSpec((1, 1, 2, lc, HEAD_DIM), lambda p, bi: (bi, ia, p, 0, 0))
    return pl.pallas_call(
        functools.partial(_na_latent_kernel, rows=rows),
        grid=(H_NA // 2, b),
        in_specs=[blk(), blk(), blk(), ctx(), ctx(),
                  pl.BlockSpec((2,) + bias.shape[1:], lambda p, bi: (p, 0, 0, 0))],
        out_specs=blk(),
        out_shape=jax.ShapeDtypeStruct((b, l, H_NA * HEAD_DIM), BF16),
        compiler_params=_cparams("parallel", "parallel"),
        name="na_latent",
    )(q, k, v, cache_k, cache_v, bias)


def _outproj_kernel(*refs, n_in, gate_i):
    a_refs = refs[:n_in]
    w_ref, x_ref, g_ref, mod_ref, o_ref = refs[n_in:]
    y = None
    k0 = 0
    for a_ref in a_refs:
        kw = a_ref.shape[-1]
        part = _dot(a_ref[...], w_ref[k0:k0 + kw, :])
        y = part if y is None else y + part
        k0 += kw
    m = mod_ref[0]
    o_ref[...] = x_ref[...] + m[gate_i:gate_i + 1] * _rms(y, g_ref[...])


def _outproj_residual(acts, w, x, g, mod, tokens_per_row, gate_i):
    t = x.shape[0]
    tm = min(512, t)
    in_specs = [pl.BlockSpec((tm, a.shape[1]), lambda i: (i, 0)) for a in acts]
    in_specs += [
        pl.BlockSpec(w.shape, lambda i: (0, 0)),
        pl.BlockSpec((tm, D_MODEL), lambda i: (i, 0)),
        pl.BlockSpec((1, D_MODEL), lambda i: (0, 0)),
        _mod_spec(tokens_per_row // tm),
    ]
    return pl.pallas_call(
        functools.partial(_outproj_kernel, n_in=len(acts), gate_i=gate_i),
        grid=(t // tm,),
        in_specs=in_specs,
        out_specs=pl.BlockSpec((tm, D_MODEL), lambda i: (i, 0)),
        out_shape=jax.ShapeDtypeStruct((t, D_MODEL), F32),
        compiler_params=_cparams("parallel"),
        name="outproj_residual",
    )(*acts, w, x, g, mod)


def _mlp_kernel(x_ref, gin_ref, gout_ref, mod_ref, w1_ref, w2_ref, o_ref, h_ref, acc_ref):
    f = pl.program_id(1)

    @pl.when(f == 0)
    def _():
        m = mod_ref[0]
        h_ref[...] = _modulate(x_ref[...], gin_ref[...], m[3:4], m[4:5]).astype(BF16)

    a = jnp.maximum(_dot(h_ref[...], w1_ref[...]), 0.0)
    part = _dot((a * a).astype(BF16), w2_ref[...])

    @pl.when(f == 0)
    def _():
        acc_ref[...] = part

    @pl.when(f > 0)
    def _():
        acc_ref[...] += part

    @pl.when(f == pl.num_programs(1) - 1)
    def _():
        m = mod_ref[0]
        o_ref[...] = x_ref[...] + m[5:6] * _rms(acc_ref[...], gout_ref[...])


def _mlp_residual(x, g_in, g_out, mod, w1, w2, tokens_per_row):
    t = x.shape[0]
    tm = min(1024, t)
    tf = 1024
    return pl.pallas_call(
        _mlp_kernel,
        grid=(t // tm, D_FF // tf),
        in_specs=[
            pl.BlockSpec((tm, D_MODEL), lambda i, f: (i, 0)),
            pl.BlockSpec((1, D_MODEL), lambda i, f: (0, 0)),
            pl.BlockSpec((1, D_MODEL), lambda i, f: (0, 0)),
            pl.BlockSpec((1, 6, D_MODEL), lambda i, f: (i // (tokens_per_row // tm), 0, 0)),
            pl.BlockSpec((D_MODEL, tf), lambda i, f: (0, f)),
            pl.BlockSpec((tf, D_MODEL), lambda i, f: (f, 0)),
        ],
        out_specs=pl.BlockSpec((tm, D_MODEL), lambda i, f: (i, 0)),
        out_shape=jax.ShapeDtypeStruct((t, D_MODEL), F32),
        scratch_shapes=[pltpu.VMEM((tm, D_MODEL), BF16), pltpu.VMEM((tm, D_MODEL), F32)],
        compiler_params=_cparams("parallel", "arbitrary"),
        name="mlp_residual",
    )(x, g_in, g_out, mod, w1, w2)


def _value_ones(n):
    lane = lax.broadcasted_iota(jnp.int32, (1, n), 1)
    return jnp.where((lane & HEAD_DIM) != 0, 1.0, 0.0)


def _rope_lanes(y, table, keep):
    lane = lax.broadcasted_iota(jnp.int32, y.shape, 1)
    prod = y * table
    rot = prod + pltpu.roll(prod, LANES - MLA_ROPE, 1)
    return jnp.where((lane >= MLA_NOPE) & (lane < MLA_NOPE + MLA_ROPE), rot, keep)


def _proj_mla_kernel(*refs, rope):
    (x_ref, g_ref, mod_ref, win_ref, qn_ref, kvn_ref, wuq_ref, wuk_ref, wuv_ref) = refs[:9]
    refs = refs[9:]
    if rope:
        tab_ref, refs = refs[0], refs[1:]
        qlat_ref, refs = refs[0], refs[1:]
    qctx_ref, k_ref, v_ref, ckv_ref, kpe_ref = refs

    m = mod_ref[0]
    h = _modulate(x_ref[...], g_ref[...], m[0:1], m[1:2]).astype(BF16)
    y = _dot(h, win_ref[...])
    c_q = y[:, :MLA_Q_LORA]
    c_kv = _rms(y[:, MLA_Q_LORA:MLA_Q_LORA + MLA_KV_LORA], kvn_ref[...])
    pe_blk = y[:, MLA_Q_LORA + MLA_KV_LORA:]
    ckv_ref[...] = c_kv
    kpe_ref[...] = pe_blk[:, MLA_NOPE:MLA_NOPE + MLA_ROPE]

    lane = lax.broadcasted_iota(jnp.int32, pe_blk.shape, 1)
    if rope:
        table = tab_ref[...]
        k_pe = _rope_lanes(pe_blk, table, jnp.zeros_like(pe_blk))
    else:
        k_pe = jnp.where((lane >= MLA_NOPE) & (lane < MLA_NOPE + MLA_ROPE), pe_blk, 0.0)

    q = _dot(_rms(c_q, qn_ref[...]).astype(BF16), wuq_ref[...])
    c_kv_b = c_kv.astype(BF16)
    kn = _dot(c_kv_b, wuk_ref[...])
    v_ref[...] = (_dot(c_kv_b, wuv_ref[...]) + _value_ones(v_ref.shape[-1])).astype(v_ref.dtype)
    for hd in range(MLA_HEADS):
        lanes = slice(hd * MLA_QK_PAD, (hd + 1) * MLA_QK_PAD)
        qh = q[:, lanes]
        qctx_ref[:, lanes] = qh.astype(qctx_ref.dtype)
        if rope:
            qlat_ref[:, lanes] = _rope_lanes(qh, table, qh).astype(qlat_ref.dtype)
        k_ref[:, lanes] = (kn[:, lanes] + k_pe).astype(k_ref.dtype)


def _proj_mla(x, g, mod, w_in, q_norm, kv_norm, w_uq, w_uk, w_uv, tokens_per_row, rope_table):
    t = x.shape[0]
    tm = min(512, t)
    rope = rope_table is not None
    full = lambda a: pl.BlockSpec(a.shape, lambda i: (0,) * a.ndim)
    tok = lambda n: pl.BlockSpec((tm, n), lambda i: (i, 0))
    in_specs = [tok(D_MODEL), full(g), _mod_spec(tokens_per_row // tm), full(w_in), full(q_norm),
                full(kv_norm), full(w_uq), full(w_uk), full(w_uv)]
    args = [x, g, mod, w_in, q_norm, kv_norm, w_uq, w_uk, w_uv]
    qk = MLA_HEADS * MLA_QK_PAD
    out_specs, out_shape = [], []
    if rope:
        tiles_per_seq = rope_table.shape[0] // tm
        in_specs.append(pl.BlockSpec((tm, LANES), lambda i: (i % tiles_per_seq, 0)))
        args.append(rope_table)
        out_specs.append(tok(qk))
        out_shape.append(jax.ShapeDtypeStruct((t, qk), BF16))
    out_specs += [tok(qk), tok(qk), tok(MLA_HEADS * HEAD_PAIR), tok(MLA_KV_LORA), tok(MLA_ROPE)]
    out_shape += [jax.ShapeDtypeStruct((t, qk), BF16), jax.ShapeDtypeStruct((t, qk), BF16),
                  jax.ShapeDtypeStruct((t, MLA_HEADS * HEAD_PAIR), BF16),
                  jax.ShapeDtypeStruct((t, MLA_KV_LORA), F32), jax.ShapeDtypeStruct((t, MLA_ROPE), F32)]
    return pl.pallas_call(
        functools.partial(_proj_mla_kernel, rope=rope),
        grid=(t // tm,),
        in_specs=in_specs,
        out_specs=out_specs,
        out_shape=out_shape,
        compiler_params=_cparams("parallel"),
        name="proj_mla",
    )(*args)


def _ctx_kv_kernel(ckv_ref, kpe_ref, wuk_ref, wuv_ref, place_ref, k_ref, v_ref):
    c = ckv_ref[...].astype(BF16)
    kn = _dot(c, wuk_ref[...])
    k_pe = _dot(kpe_ref[...].astype(BF16), place_ref[...])
    v_ref[...] = (_dot(c, wuv_ref[...]) + _value_ones(v_ref.shape[-1])).astype(v_ref.dtype)
    for hd in range(MLA_HEADS):
        lanes = slice(hd * MLA_QK_PAD, (hd + 1) * MLA_QK_PAD)
        k_ref[:, lanes] = (kn[:, lanes] + k_pe).astype(k_ref.dtype)


def _ctx_kv(ckv, kpe, w_uk, w_uv, place):
    t = ckv.shape[0]
    tm = min(512, t)
    full = lambda a: pl.BlockSpec(a.shape, lambda i: (0,) * a.ndim)
    tok = lambda n: pl.BlockSpec((tm, n), lambda i: (i, 0))
    return pl.pallas_call(
        _ctx_kv_kernel,
        grid=(t // tm,),
        in_specs=[tok(MLA_KV_LORA), tok(MLA_ROPE), full(w_uk), full(w_uv), full(place)],
        out_specs=[tok(MLA_HEADS * MLA_QK_PAD), tok(MLA_HEADS * HEAD_PAIR)],
        out_shape=[jax.ShapeDtypeStruct((t, MLA_HEADS * MLA_QK_PAD), BF16),
                   jax.ShapeDtypeStruct((t, MLA_HEADS * HEAD_PAIR), BF16)],
        compiler_params=_cparams("parallel"),
        name="mla_ctx_kv",
    )(ckv, kpe, w_uk, w_uv, place)


def _pair_swap(w):
    return w.reshape(w.shape[:-1] + (w.shape[-1] // 2, 2))[..., ::-1].reshape(w.shape)


def _mla_weights(w_in, w_uq, w_uk, w_uv):
    k_pe_cols = w_in[:, MLA_Q_LORA + MLA_KV_LORA:]
    w_in_ext = jnp.concatenate(
        [w_in[:, :MLA_Q_LORA + MLA_KV_LORA], jnp.zeros((D_MODEL, MLA_NOPE), w_in.dtype),
         k_pe_cols, _pair_swap(k_pe_cols)], axis=1).astype(BF16)
    uq = w_uq.reshape(MLA_Q_LORA, MLA_HEADS, MLA_NOPE + MLA_ROPE)
    uq_ext = jnp.concatenate([uq, _pair_swap(uq[..., MLA_NOPE:])], axis=-1)
    uq_ext = uq_ext.reshape(MLA_Q_LORA, MLA_HEADS * MLA_QK_PAD).astype(BF16)
    uk = w_uk.reshape(MLA_KV_LORA, MLA_HEADS, MLA_NOPE)
    uk_ext = jnp.concatenate([uk, jnp.zeros((MLA_KV_LORA, MLA_HEADS, MLA_QK_PAD - MLA_NOPE), uk.dtype)], axis=-1)
    uk_ext = uk_ext.reshape(MLA_KV_LORA, MLA_HEADS * MLA_QK_PAD).astype(BF16)
    uv = w_uv.reshape(MLA_KV_LORA, MLA_HEADS, MLA_V)
    uv_ext = jnp.concatenate([uv, jnp.zeros((MLA_KV_LORA, MLA_HEADS, HEAD_PAIR - MLA_V), uv.dtype)], axis=-1)
    uv_ext = uv_ext.reshape(MLA_KV_LORA, MLA_HEADS * HEAD_PAIR).astype(BF16)
    return w_in_ext, uq_ext, uk_ext, uv_ext


def _rope_table(l):
    t = jnp.arange(l)
    row = (t // GRID_W).astype(F32)
    col = (t % GRID_W).astype(F32)
    nf = MLA_ROPE // 4
    inv = ROPE_BASE ** (-jnp.arange(nf, dtype=F32) / nf)
    ang = jnp.concatenate([row[:, None] * inv, col[:, None] * inv], axis=-1)
    cos, sin = jnp.cos(ang), jnp.sin(ang)
    cc = jnp.repeat(cos, 2, axis=-1)
    ss = jnp.stack([-sin, sin], axis=-1).reshape(l, MLA_ROPE)
    return jnp.concatenate([jnp.ones((l, MLA_NOPE), F32), cc, ss], axis=-1)


def kernel(x_prompt, x_sample, state_ret_fwd, state_ret_bwd, cache_na_k, cache_na_v, cache_mla_ckv,
           cache_mla_kpe, c, c_ctx, w_ada, b_ada, norm_gains, w_mlp_in, w_mlp_out, w_in_ac, w_out_ac,
           ret_decay_fwd, ret_decay_bwd, na_rpb, w_in_c, mla_q_norm, mla_kv_norm, w_uq, w_uk, w_uv, w_out_c):
    bp, lp, _ = x_prompt.shape
    bs, ls, _ = x_sample.shape
    lc = cache_mla_ckv.shape[2]
    xp = x_prompt.reshape(bp * lp, D_MODEL)
    xs = x_sample.reshape(bs * ls, D_MODEL)

    n_cond = bs + 1
    cond = jnp.concatenate([c, c_ctx[None]], axis=0)
    mods = _modulation_all(cond, w_ada, b_ada)
    rope_table = _rope_table(ls)
    place = jnp.zeros((MLA_ROPE, MLA_QK_PAD), BF16).at[
        jnp.arange(MLA_ROPE), MLA_NOPE + jnp.arange(MLA_ROPE)].set(1.0)

    ret_f, ret_b, na_k, na_v, mla_ckv, mla_kpe = [], [], [], [], [], []
    for layer in range(DEPTH):
        mod_s = mods[layer, :bs]
        mod_p = mods[layer, bs:n_cond]
        g = norm_gains[layer]
        g0, g1, g2, g3 = (g[i:i + 1] for i in range(4))
        tp, ts = bp * lp, ls
        if layer % 2 == 0:
            ia = layer // 2
            w_in = w_in_ac[ia].astype(BF16)
            w_out = w_out_ac[ia].astype(BF16)
            lg_f = jnp.log1p(-jnp.exp2(ret_decay_fwd[ia].astype(F32)))
            lg_b = jnp.log1p(-jnp.exp2(ret_decay_bwd[ia].astype(F32)))
            seq = lambda a, b, l: a.reshape(b, l, a.shape[-1])

            rq, rk, rv, rg, nq, nk, nv = [seq(a, bp, lp) for a in _proj_even(xp, g0, mod_p, w_in, tp, F32)]
            o_ret, s_f, s_b = _retention(rq, rk, rv, rg, lg_f, lg_b, None, None, ia, True)
            o_na, k_out, v_out = _attention([nq], [nk], [nv], H_NA, HEAD_DIM, HEAD_DIM ** -0.5, lp,
                                             heads_per_step=H_NA, emit_kv=True)
            ret_f.append(s_f)
            ret_b.append(s_b)
            na_k.append(k_out)
            na_v.append(v_out)
            xp = _outproj_residual([o_ret.reshape(tp, -1), o_na.reshape(tp, -1)], w_out, xp, g1, mod_p, tp, 2)

            rq, rk, rv, rg, nq, nk, nv = [seq(a, bs, ls) for a in _proj_even(xs, g0, mod_s, w_in, ts, BF16)]
            o_ret = _retention(rq, rk, rv, rg, lg_f, lg_b, state_ret_fwd, state_ret_bwd, ia, False)[0]
            o_na = _na_latent(nq, nk, nv, cache_na_k, cache_na_v, ia, _na_bias_tables(na_rpb[ia], ls // GRID_W))
            xs = _outproj_residual([o_ret.reshape(bs * ls, -1), o_na.reshape(bs * ls, -1)], w_out, xs, g1,
                                   mod_s, ts, 2)
        else:
            ic = layer // 2
            w_in_ext, uq_ext, uk_ext, uv = _mla_weights(w_in_c[ic], w_uq[ic], w_uk[ic], w_uv[ic])
            w_out = w_out_c[ic].astype(BF16)
            qn, kvn = mla_q_norm[ic][None], mla_kv_norm[ic][None]

            q, k, v, ckv, kpe = _proj_mla(xp, g0, mod_p, w_in_ext, qn, kvn, uq_ext, uk_ext, uv, tp, None)
            mla_ckv.append(ckv.reshape(bp, lp, MLA_KV_LORA))
            mla_kpe.append(kpe.reshape(bp, lp, MLA_ROPE))
            sq = lambda a, b, l: a.reshape(b, l, a.shape[-1])
            o = _attention_aug([sq(q, bp, lp)], [sq(k, bp, lp)], [sq(v, bp, lp)], MLA_HEADS, MLA_QK_PAD,
                               MLA_SCALE, lp, heads_per_step=8)
            xp = _outproj_residual([o.reshape(tp, -1)], w_out, xp, g1, mod_p, tp, 2)

            q_lat, q_ctx, k, v, _, _ = _proj_mla(xs, g0, mod_s, w_in_ext, qn, kvn, uq_ext, uk_ext, uv, ts,
                                                 rope_table)
            k_c, v_c = _ctx_kv(cache_mla_ckv[:, ic].reshape(bs * lc, MLA_KV_LORA),
                               cache_mla_kpe[:, ic].reshape(bs * lc, MLA_ROPE), uk_ext, uv, place)
            o = _attention_aug([sq(q_lat, bs, ls), sq(q_ctx, bs, ls)], [sq(k, bs, ls), sq(k_c, bs, lc)],
                               [sq(v, bs, ls), sq(v_c, bs, lc)], MLA_HEADS, MLA_QK_PAD, MLA_SCALE, 256)
            xs = _outproj_residual([o.reshape(bs * ls, -1)], w_out, xs, g1, mod_s, ts, 2)

        w1 = w_mlp_in[layer].astype(BF16)
        w2 = w_mlp_out[layer].astype(BF16)
        xp = _mlp_residual(xp, g2, g3, mod_p, w1, w2, tp)
        xs = _mlp_residual(xs, g2, g3, mod_s, w1, w2, ts)

    return (xp.reshape(bp, lp, D_MODEL), xs.reshape(bs, ls, D_MODEL),
            jnp.stack(ret_f, axis=1), jnp.stack(ret_b, axis=1),
            jnp.stack(na_k, axis=1), jnp.stack(na_v, axis=1),
            jnp.stack(mla_ckv, axis=1), jnp.stack(mla_kpe, axis=1))
```

```python
import functools

import numpy as np
import jax
import jax.numpy as jnp
from jax import lax
from jax.experimental import pallas as pl
from jax.experimental.pallas import tpu as pltpu

D_MODEL = 1024
DEPTH = 4
GRID_W = 64
HEAD_DIM = 64
N_HEADS = D_MODEL // HEAD_DIM
H_RET = N_HEADS // 2
H_NA = N_HEADS - H_RET
RET_CHUNK = 128
WIN_R = 8
WIN_C = 16
MLA_HEADS = N_HEADS
MLA_Q_LORA = 384
MLA_KV_LORA = 256
MLA_NOPE = 64
MLA_ROPE = 32
MLA_V = 64
MLA_SCALE = (MLA_NOPE + MLA_ROPE) ** -0.5
ROPE_BASE = 10000.0
D_FF = 4 * D_MODEL
EPS = 1e-6
NEG_INF = -1e30

LANES = 128
HEAD_PAIR = 2 * HEAD_DIM
MLA_QK_PAD = 128
VMEM_LIMIT = 56 * 1024 * 1024

F32 = jnp.float32
BF16 = jnp.bfloat16


def _cparams(*sem):
    return pltpu.CompilerParams(dimension_semantics=sem, vmem_limit_bytes=VMEM_LIMIT)


def _dot(a, b):
    return jnp.dot(a, b, preferred_element_type=F32)


def _dot_nt(a, b):
    return lax.dot_general(a, b, (((1,), (1,)), ((), ())), preferred_element_type=F32)


def _dot_tn(a, b):
    return lax.dot_general(a, b, (((0,), (0,)), ((), ())), preferred_element_type=F32)


def _rms(x, g):
    ms = jnp.mean(x * x, axis=-1, keepdims=True)
    return x * lax.rsqrt(ms + EPS) * g


def _modulate(x, g, shift, scale):
    return _rms(x, g) * (1.0 + scale) + shift


def _mod_kernel(c_ref, w_ref, b_ref, o_ref):
    c = c_ref[...]
    s = c / (1.0 + jnp.exp(-c))
    o_ref[0] = _dot(s.astype(BF16), w_ref[0].astype(BF16)) + b_ref[0]


def _modulation_all(cond, w_ada, b_ada):
    r = cond.shape[0]
    tn = 1536
    out = pl.pallas_call(
        _mod_kernel,
        grid=(DEPTH, 6 * D_MODEL // tn),
        in_specs=[
            pl.BlockSpec((r, D_MODEL), lambda l, j: (0, 0)),
            pl.BlockSpec((1, D_MODEL, tn), lambda l, j: (l, 0, j)),
            pl.BlockSpec((1, 1, tn), lambda l, j: (l, 0, j)),
        ],
        out_specs=pl.BlockSpec((1, r, tn), lambda l, j: (l, 0, j)),
        out_shape=jax.ShapeDtypeStruct((DEPTH, r, 6 * D_MODEL), F32),
        compiler_params=_cparams("parallel", "parallel"),
        name="modulation",
    )(cond, w_ada, b_ada.reshape(DEPTH, 1, 6 * D_MODEL))
    return out.reshape(DEPTH, r, 6, D_MODEL)


def _mod_spec(tiles_per_row):
    return pl.BlockSpec((1, 6, D_MODEL), lambda i: (i // tiles_per_row, 0, 0))


def _proj_even_kernel(x_ref, g_ref, mod_ref, w_ref, *out_refs, mults):
    m = mod_ref[0]
    h = _modulate(x_ref[...], g_ref[...], m[0:1], m[1:2]).astype(BF16)
    width = out_refs[0].shape[-1]
    for i, (o_ref, mult) in enumerate(zip(out_refs, mults)):
        y = _dot(h, w_ref[:, i * width:(i + 1) * width])
        if mult != 1.0:
            y = y * mult
        o_ref[...] = y.astype(o_ref.dtype)


def _proj_even(x, g, mod, w, tokens_per_row, kv_dtype):
    t = x.shape[0]
    tm = min(512, t)
    width = H_RET * HEAD_DIM
    dtypes = [BF16, BF16, BF16, F32, BF16, kv_dtype, kv_dtype]
    mults = (1.0, HEAD_DIM ** -0.5, 1.0, 1.0, 1.0, 1.0, 1.0)
    return pl.pallas_call(
        functools.partial(_proj_even_kernel, mults=mults),
        grid=(t // tm,),
        in_specs=[
            pl.BlockSpec((tm, D_MODEL), lambda i: (i, 0)),
            pl.BlockSpec((1, D_MODEL), lambda i: (0, 0)),
            _mod_spec(tokens_per_row // tm),
            pl.BlockSpec(w.shape, lambda i: (0, 0)),
        ],
        out_specs=[pl.BlockSpec((tm, width), lambda i: (i, 0)) for _ in dtypes],
        out_shape=[jax.ShapeDtypeStruct((t, width), dt) for dt in dtypes],
        compiler_params=_cparams("parallel"),
        name="proj_even",
    )(x, g, mod, w)


def _retention_kernel(lgf_ref, lgb_ref, q_ref, k_ref, v_ref, g_ref, *rest, n_chunks, has_state, emit_state):
    if has_state:
        sf0_ref, sb0_ref = rest[0], rest[1]
        rest = rest[2:]
    o_ref = rest[0]
    rest = rest[1:]
    if emit_state:
        sfo_ref, sbo_ref = rest[0], rest[1]
        rest = rest[2:]
    acc_ref, kv_ref, st_ref, tab_ref = rest

    c = RET_CHUNK
    d = HEAD_DIM
    pair = pl.program_id(0)
    first_half = lax.broadcasted_iota(jnp.int32, (c, HEAD_PAIR), 1) < d

    @pl.when(pl.program_id(1) == 0)
    def _():
        row = lax.broadcasted_iota(jnp.int32, (c, c), 0).astype(F32)
        col = lax.broadcasted_iota(jnp.int32, (c, c), 1).astype(F32)
        diff = row - col
        pos = lax.broadcasted_iota(jnp.int32, (c, HEAD_PAIR), 0).astype(F32)
        for j in range(2):
            lgf = lgf_ref[2 * pair + j]
            lgb = lgb_ref[2 * pair + j]
            tab_ref[j, 0] = (jnp.where(diff >= 0, jnp.exp(lgf * jnp.maximum(diff, 0.0)), 0.0)
                             + jnp.where(diff <= 0, jnp.exp(lgb * jnp.maximum(-diff, 0.0)), 0.0))
            tab_ref[j, 1] = jnp.where(first_half, jnp.exp(lgf * (c - 1 - pos)), jnp.exp(lgb * pos))
            tab_ref[j, 2] = jnp.where(first_half, jnp.exp(lgf * (pos + 1.0)), jnp.exp(lgb * (c - pos)))

    cdecs = []
    for j in range(2):
        lgf = lgf_ref[2 * pair + j]
        lgb = lgb_ref[2 * pair + j]
        cdecs.append((jnp.exp(jnp.full((1, 1), lgf * c, F32)), jnp.exp(jnp.full((1, 1), lgb * c, F32))))

    def doubled(tile):
        t = tile.astype(F32)
        r = pltpu.roll(t, d, 1)
        return jnp.where(first_half, t, r), jnp.where(first_half, r, t)

    def chunk_rows(n):
        return pl.ds(pl.multiple_of(n * c, c), c)

    halves = (first_half, jnp.logical_not(first_half))
    same_head = ((lax.broadcasted_iota(jnp.int32, (HEAD_PAIR, HEAD_PAIR), 0) < d)
                 == (lax.broadcasted_iota(jnp.int32, (HEAD_PAIR, HEAD_PAIR), 1) < d))
    head_mean = jnp.where(same_head, 1.0 / d, 0.0).astype(BF16)

    def group_mean(x):
        hi = x.astype(BF16)
        lo = (x - hi.astype(F32)).astype(BF16)
        return _dot(hi, head_mean) + _dot(lo, head_mean)

    def intra(n, carry):
        rows = chunk_rows(n)
        q, k, v = q_ref[0, rows, :], k_ref[0, rows, :], v_ref[0, rows, :]
        kf = k.astype(F32)
        k2 = doubled(k)
        o = None
        for j in range(2):
            kj = jnp.where(halves[j], kf, 0.0).astype(BF16)
            s = (_dot_nt(q, kj) * tab_ref[j, 0]).astype(BF16)
            oj = _dot(s, v)
            o = oj if o is None else jnp.where(first_half, o, oj)
            kv = _dot_tn((k2[j] * tab_ref[j, 1]).astype(BF16), v)
            kv_ref[n, j] = jnp.where(halves[j], kv, 0.0)
        acc_ref[rows, :] = o
        return carry

    lax.fori_loop(0, n_chunks, intra, 0, unroll=min(4, n_chunks))

    for j in range(2):
        zeros = jnp.zeros((d, d), F32)
        if has_state:
            place = (lambda s: jnp.concatenate([s, zeros], axis=1)) if j == 0 else (
                lambda s: jnp.concatenate([zeros, s], axis=1))
            sf0, sb0 = place(sf0_ref[0, 0, j]), place(sb0_ref[0, 0, j])
        else:
            sf0 = sb0 = jnp.zeros((d, HEAD_PAIR), F32)

        def scan_f(n, state):
            st_ref[n, 2 * j * d:(2 * j + 1) * d, :] = state.astype(BF16)
            return cdecs[j][0] * state + kv_ref[n, j, 0:d, :]

        def scan_b(i, state):
            n = n_chunks - 1 - i
            st_ref[n, (2 * j + 1) * d:(2 * j + 2) * d, :] = state.astype(BF16)
            return cdecs[j][1] * state + kv_ref[n, j, d:2 * d, :]

        fin_f = lax.fori_loop(0, n_chunks, scan_f, sf0)
        fin_b = lax.fori_loop(0, n_chunks, scan_b, sb0)
        if emit_state:
            sfo_ref[0, j] = fin_f[:, j * d:(j + 1) * d]
            sbo_ref[0, j] = fin_b[:, j * d:(j + 1) * d]

    def finish(n, carry):
        rows = chunk_rows(n)
        q2 = doubled(q_ref[0, rows, :])
        qd = jnp.concatenate([(q2[j] * tab_ref[j, 2]).astype(BF16) for j in range(2)], axis=1)
        o = acc_ref[rows, :] + _dot(qd, st_ref[n])
        cen = o - group_mean(o)
        on = cen * lax.rsqrt(group_mean(cen * cen) + EPS)
        gate = g_ref[0, rows, :]
        o_ref[0, rows, :] = (on * (gate / (1.0 + jnp.exp(-gate)))).astype(o_ref.dtype)
        return carry

    lax.fori_loop(0, n_chunks, finish, 0, unroll=min(4, n_chunks))


def _retention(q, k, v, g, lg_f, lg_b, state_f, state_b, ia, emit_state):
    b, l, _ = q.shape
    has_state = state_f is not None
    blk = lambda: pl.BlockSpec((1, l, HEAD_PAIR), lambda p, bi: (bi, 0, p))
    smem = pl.BlockSpec(memory_space=pltpu.SMEM)
    in_specs = [smem, smem, blk(), blk(), blk(), blk()]
    args = [lg_f, lg_b, q, k, v, g]
    if has_state:
        st = lambda: pl.BlockSpec((1, 1, 2, HEAD_DIM, HEAD_DIM), lambda p, bi: (bi, ia, p, 0, 0))
        in_specs += [st(), st()]
        args += [state_f, state_b]
    out_specs = [blk()]
    out_shape = [jax.ShapeDtypeStruct((b, l, H_RET * HEAD_DIM), BF16)]
    if emit_state:
        so = lambda: pl.BlockSpec((1, 2, HEAD_DIM, HEAD_DIM), lambda p, bi: (bi, p, 0, 0))
        out_specs += [so(), so()]
        out_shape += [jax.ShapeDtypeStruct((b, H_RET, HEAD_DIM, HEAD_DIM), F32)] * 2
    return pl.pallas_call(
        functools.partial(_retention_kernel, n_chunks=l // RET_CHUNK, has_state=has_state,
                          emit_state=emit_state),
        grid=(H_RET // 2, b),
        in_specs=in_specs,
        out_specs=out_specs,
        out_shape=out_shape,
        scratch_shapes=[pltpu.VMEM((l, HEAD_PAIR), F32),
                        pltpu.VMEM((l // RET_CHUNK, 2, HEAD_PAIR, HEAD_PAIR), F32),
                        pltpu.VMEM((l // RET_CHUNK, 2 * HEAD_PAIR, HEAD_PAIR), BF16),
                        pltpu.VMEM((2, 3, RET_CHUNK, HEAD_PAIR), F32)],
        compiler_params=_cparams("arbitrary", "arbitrary"),
        name="retention",
    )(*args)


LOG2_E = 1.4426950408889634


def _attn_aug_kernel(*refs, nseg, dq, scale, heads):
    q_refs = refs[0:nseg]
    k_refs = refs[nseg:2 * nseg]
    v_refs = refs[2 * nseg:3 * nseg]
    o_ref = refs[3 * nseg]
    first_half = lax.broadcasted_iota(jnp.int32, (o_ref.shape[1], HEAD_PAIR), 1) < HEAD_DIM
    for pair in range(heads // 2):
        res = []
        for j in range(2 * pair, 2 * pair + 2):
            ql = slice(j * dq, (j + 1) * dq)
            vl = slice(j * HEAD_PAIR, (j + 1) * HEAD_PAIR)
            scores = [_dot_nt(q_refs[s][0, :, ql], k_refs[s][0, :, ql]) for s in range(nseg)]
            m = functools.reduce(jnp.maximum, [jnp.max(s, axis=-1, keepdims=True) for s in scores])
            acc = None
            for s in range(nseg):
                p = jnp.exp2((scores[s] - m) * (scale * LOG2_E)).astype(BF16)
                pv = _dot(p, v_refs[s][0, :, vl])
                acc = pv if acc is None else acc + pv
            res.append(acc / pltpu.roll(acc, HEAD_DIM, 1))
        out = jnp.where(first_half, res[0], pltpu.roll(res[1], HEAD_DIM, 1))
        o_ref[0, :, pair * HEAD_PAIR:(pair + 1) * HEAD_PAIR] = out.astype(o_ref.dtype)


def _attention_aug(qs, ks, vs, n_heads, dq, scale, tq, heads_per_step=2):
    nseg = len(qs)
    b, lq, _ = qs[0].shape
    hs = heads_per_step
    in_specs = [pl.BlockSpec((1, tq, hs * dq), lambda bi, p, qi: (bi, qi, p)) for _ in qs]
    in_specs += [pl.BlockSpec((1, k.shape[1], hs * dq), lambda bi, p, qi: (bi, 0, p)) for k in ks]
    in_specs += [pl.BlockSpec((1, v.shape[1], hs * HEAD_PAIR), lambda bi, p, qi: (bi, 0, p)) for v in vs]
    return pl.pallas_call(
        functools.partial(_attn_aug_kernel, nseg=nseg, dq=dq, scale=scale, heads=hs),
        grid=(b, n_heads // hs, lq // tq),
        in_specs=in_specs,
        out_specs=pl.BlockSpec((1, tq, hs * HEAD_DIM), lambda bi, p, qi: (bi, qi, p)),
        out_shape=jax.ShapeDtypeStruct((b, lq, n_heads * HEAD_DIM), BF16),
        compiler_params=_cparams("parallel", "parallel", "arbitrary"),
        name="attention_mla",
    )(*qs, *ks, *vs)


def _attn_kernel(*refs, nseg, dq, scale, heads, emit_kv, v_stride):
    q_refs = refs[0:nseg]
    k_refs = refs[nseg:2 * nseg]
    v_refs = refs[2 * nseg:3 * nseg]
    o_ref = refs[3 * nseg]
    for pair in range(heads // 2):
        outs = []
        for j in range(2 * pair, 2 * pair + 2):
            ql = slice(j * dq, (j + 1) * dq)
            vl = slice(j * v_stride, j * v_stride + HEAD_DIM)
            scores = [_dot_nt(q_refs[s][0, :, ql], k_refs[s][0, :, ql].astype(BF16)) for s in range(nseg)]
            m = functools.reduce(jnp.maximum, [jnp.max(s, axis=-1, keepdims=True) for s in scores])
            acc = None
            den = None
            for s in range(nseg):
                p = jnp.exp((scores[s] - m) * scale)
                ps = jnp.sum(p, axis=-1, keepdims=True)
                pv = _dot(p.astype(BF16), v_refs[s][0, :, vl].astype(BF16))
                acc = pv if acc is None else acc + pv
                den = ps if den is None else den + ps
            outs.append(acc / den)
        o_ref[0, :, pair * HEAD_PAIR:(pair + 1) * HEAD_PAIR] = jnp.concatenate(outs, axis=-1).astype(o_ref.dtype)
    if emit_kv:
        ko_ref, vo_ref = refs[3 * nseg + 1], refs[3 * nseg + 2]
        for j in range(heads):
            ko_ref[0, j] = k_refs[0][0, :, j * dq:(j + 1) * dq]
            vo_ref[0, j] = v_refs[0][0, :, j * HEAD_DIM:(j + 1) * HEAD_DIM]


def _attention(qs, ks, vs, n_heads, dq, scale, tq, heads_per_step=2, emit_kv=False, v_stride=HEAD_DIM):
    nseg = len(qs)
    b, lq, _ = qs[0].shape
    hs = heads_per_step
    in_specs = [pl.BlockSpec((1, tq, hs * dq), lambda bi, p, qi: (bi, qi, p)) for _ in qs]
    in_specs += [pl.BlockSpec((1, k.shape[1], hs * dq), lambda bi, p, qi: (bi, 0, p)) for k in ks]
    in_specs += [pl.BlockSpec((1, v.shape[1], hs * v_stride), lambda bi, p, qi: (bi, 0, p)) for v in vs]
    out_specs = [pl.BlockSpec((1, tq, hs * HEAD_DIM), lambda bi, p, qi: (bi, qi, p))]
    out_shape = [jax.ShapeDtypeStruct((b, lq, n_heads * HEAD_DIM), BF16)]
    if emit_kv:
        assert nseg == 1 and tq == lq and dq == HEAD_DIM
        lk = ks[0].shape[1]
        kv = lambda: pl.BlockSpec((1, hs, lk, HEAD_DIM), lambda bi, p, qi: (bi, p, 0, 0))
        out_specs += [kv(), kv()]
        out_shape += [jax.ShapeDtypeStruct((b, n_heads, lk, HEAD_DIM), F32)] * 2
    res = pl.pallas_call(
        functools.partial(_attn_kernel, nseg=nseg, dq=dq, scale=scale, heads=hs, emit_kv=emit_kv,
                          v_stride=v_stride),
        grid=(b, n_heads // hs, lq // tq),
        in_specs=in_specs,
        out_specs=out_specs,
        out_shape=out_shape,
        compiler_params=_cparams("parallel", "parallel", "arbitrary"),
        name="attention",
    )(*qs, *ks, *vs)
    return res if emit_kv else res[0]


NA_QROWS = 4
NA_KROWS = NA_QROWS + WIN_R - 1


def _na_block_start(g, rows):
    return np.clip(NA_QROWS * g - WIN_R // 2, 0, rows - NA_KROWS)


def _na_bias_tables(rpb, rows):
    h = rpb.shape[0]
    w = GRID_W
    nd = 2 * WIN_R - 1
    period = 2 * w - 1
    lo = w - WIN_C
    u = jnp.pad(rpb.astype(F32), ((0, 0), (0, 0), (lo, period - lo - (2 * WIN_C - 1))))
    big = jnp.tile(u, (1, 1, w + 1))[..., :w * 2 * w].reshape(h, nd, w, 2 * w)
    colb = big[:, :, ::-1, :w]
    col = np.arange(w)
    win_start = np.clip(col - WIN_C // 2, 0, w - WIN_C)
    valid = (col[None, :] >= win_start[:, None]) & (col[None, :] < win_start[:, None] + WIN_C)
    colb = jnp.where(valid[None, None], colb, NEG_INF)
    neg = jnp.full((h, w, w), NEG_INF, F32)
    nblk = rows // NA_QROWS
    tables = []
    for g in (0, 1, nblk - 1):
        ks = _na_block_start(g, rows)
        blk_rows = []
        for rr in range(NA_QROWS):
            r = NA_QROWS * g + rr
            rs = np.clip(r - WIN_R // 2, 0, rows - WIN_R)
            pieces = []
            for jj in range(NA_KROWS):
                krow = ks + jj
                pieces.append(colb[:, krow - r + WIN_R - 1] if rs <= krow < rs + WIN_R else neg)
            blk_rows.append(jnp.concatenate(pieces, axis=-1))
        tables.append(jnp.concatenate(blk_rows, axis=1))
    return jnp.stack(tables, axis=1)


def _na_latent_kernel(q_ref, k_ref, v_ref, kc_ref, vc_ref, bias_ref, o_ref, *, rows):
    scale = HEAD_DIM ** -0.5
    nq = NA_QROWS * GRID_W
    nk = NA_KROWS * GRID_W
    nblk = rows // NA_QROWS

    def body(g, carry):
        ks = jnp.clip(NA_QROWS * g - WIN_R // 2, 0, rows - NA_KROWS)
        case = jnp.where(g == 0, 0, jnp.where(g == nblk - 1, 2, 1))
        qrows = pl.ds(pl.multiple_of(g * nq, nq), nq)
        krows = pl.ds(pl.multiple_of(ks * GRID_W, GRID_W), nk)
        outs = []
        for j in range(2):
            lanes = slice(j * HEAD_DIM, (j + 1) * HEAD_DIM)
            q = q_ref[0, qrows, lanes]
            s_loc = _dot_nt(q, k_ref[0, krows, lanes]) * scale + bias_ref[j, case]
            s_ctx = _dot_nt(q, kc_ref[0, 0, j].astype(BF16)) * scale
            m = jnp.maximum(jnp.max(s_loc, axis=-1, keepdims=True), jnp.max(s_ctx, axis=-1, keepdims=True))
            p_loc = jnp.exp(s_loc - m)
            p_ctx = jnp.exp(s_ctx - m)
            den = jnp.sum(p_loc, axis=-1, keepdims=True) + jnp.sum(p_ctx, axis=-1, keepdims=True)
            acc = (_dot(p_loc.astype(BF16), v_ref[0, krows, lanes])
                   + _dot(p_ctx.astype(BF16), vc_ref[0, 0, j].astype(BF16)))
            outs.append(acc / den)
        o_ref[0, qrows, :] = jnp.concatenate(outs, axis=-1).astype(o_ref.dtype)
        return carry

    lax.fori_loop(0, nblk, body, 0)


def _na_latent(q, k, v, cache_k, cache_v, ia, bias):
    b, l, _ = q.shape
    lc = cache_k.shape[3]
    rows = l // GRID_W
    assert rows % NA_QROWS == 0 and rows >= NA_KROWS + NA_QROWS
    blk = lambda: pl.BlockSpec((1, l, HEAD_PAIR), lambda p, bi: (bi, 0, p))
    ctx = lambda: pl.BlockSpec((1, 1, 2, lc, HEAD_DIM), lambda p, bi: (bi, ia, p, 0, 0))
    return pl.pallas_call(
        functools.partial(_na_latent_kernel, rows=rows),
        grid=(H_NA // 2, b),
        in_specs=[blk(), blk(), blk(), ctx(), ctx(),
                  pl.BlockSpec((2,) + bias.shape[1:], lambda p, bi: (p, 0, 0, 0))],
        out_specs=blk(),
        out_shape=jax.ShapeDtypeStruct((b, l, H_NA * HEAD_DIM), BF16),
        compiler_params=_cparams("parallel", "parallel"),
        name="na_latent",
    )(q, k, v, cache_k, cache_v, bias)


def _outproj_kernel(*refs, n_in, gate_i):
    a_refs = refs[:n_in]
    w_ref, x_ref, g_ref, mod_ref, o_ref = refs[n_in:]
    y = None
    k0 = 0
    for a_ref in a_refs:
        kw = a_ref.shape[-1]
        part = _dot(a_ref[...], w_ref[k0:k0 + kw, :])
        y = part if y is None else y + part
        k0 += kw
    m = mod_ref[0]
    o_ref[...] = x_ref[...] + m[gate_i:gate_i + 1] * _rms(y, g_ref[...])


def _outproj_residual(acts, w, x, g, mod, tokens_per_row, gate_i):
    t = x.shape[0]
    tm = min(512, t)
    in_specs = [pl.BlockSpec((tm, a.shape[1]), lambda i: (i, 0)) for a in acts]
    in_specs += [
        pl.BlockSpec(w.shape, lambda i: (0, 0)),
        pl.BlockSpec((tm, D_MODEL), lambda i: (i, 0)),
        pl.BlockSpec((1, D_MODEL), lambda i: (0, 0)),
        _mod_spec(tokens_per_row // tm),
    ]
    return pl.pallas_call(
        functools.partial(_outproj_kernel, n_in=len(acts), gate_i=gate_i),
        grid=(t // tm,),
        in_specs=in_specs,
        out_specs=pl.BlockSpec((tm, D_MODEL), lambda i: (i, 0)),
        out_shape=jax.ShapeDtypeStruct((t, D_MODEL), F32),
        compiler_params=_cparams("parallel"),
        name="outproj_residual",
    )(*acts, w, x, g, mod)


def _mlp_kernel(x_ref, gin_ref, gout_ref, mod_ref, w1_ref, w2_ref, o_ref, h_ref, acc_ref):
    f = pl.program_id(1)

    @pl.when(f == 0)
    def _():
        m = mod_ref[0]
        h_ref[...] = _modulate(x_ref[...], gin_ref[...], m[3:4], m[4:5]).astype(BF16)

    a = jnp.maximum(_dot(h_ref[...], w1_ref[...]), 0.0)
    part = _dot((a * a).astype(BF16), w2_ref[...])

    @pl.when(f == 0)
    def _():
        acc_ref[...] = part

    @pl.when(f > 0)
    def _():
        acc_ref[...] += part

    @pl.when(f == pl.num_programs(1) - 1)
    def _():
        m = mod_ref[0]
        o_ref[...] = x_ref[...] + m[5:6] * _rms(acc_ref[...], gout_ref[...])


def _mlp_residual(x, g_in, g_out, mod, w1, w2, tokens_per_row):
    t = x.shape[0]
    tm = min(1024, t)
    tf = 2048
    return pl.pallas_call(
        _mlp_kernel,
        grid=(t // tm, D_FF // tf),
        in_specs=[
            pl.BlockSpec((tm, D_MODEL), lambda i, f: (i, 0)),
            pl.BlockSpec((1, D_MODEL), lambda i, f: (0, 0)),
            pl.BlockSpec((1, D_MODEL), lambda i, f: (0, 0)),
            pl.BlockSpec((1, 6, D_MODEL), lambda i, f: (i // (tokens_per_row // tm), 0, 0)),
            pl.BlockSpec((D_MODEL, tf), lambda i, f: (0, f)),
            pl.BlockSpec((tf, D_MODEL), lambda i, f: (f, 0)),
        ],
        out_specs=pl.BlockSpec((tm, D_MODEL), lambda i, f: (i, 0)),
        out_shape=jax.ShapeDtypeStruct((t, D_MODEL), F32),
        scratch_shapes=[pltpu.VMEM((tm, D_MODEL), BF16), pltpu.VMEM((tm, D_MODEL), F32)],
        compiler_params=_cparams("parallel", "arbitrary"),
        name="mlp_residual",
    )(x, g_in, g_out, mod, w1, w2)


def _value_ones(n):
    lane = lax.broadcasted_iota(jnp.int32, (1, n), 1)
    return jnp.where((lane & HEAD_DIM) != 0, 1.0, 0.0)


def _rope_lanes(y, table, keep):
    lane = lax.broadcasted_iota(jnp.int32, y.shape, 1)
    prod = y * table
    rot = prod + pltpu.roll(prod, LANES - MLA_ROPE, 1)
    return jnp.where((lane >= MLA_NOPE) & (lane < MLA_NOPE + MLA_ROPE), rot, keep)


def _proj_mla_kernel(*refs, rope):
    (x_ref, g_ref, mod_ref, win_ref, qn_ref, kvn_ref, wuq_ref, wuk_ref, wuv_ref) = refs[:9]
    refs = refs[9:]
    if rope:
        tab_ref, refs = refs[0], refs[1:]
        qlat_ref, refs = refs[0], refs[1:]
    qctx_ref, k_ref, v_ref, ckv_ref, kpe_ref = refs

    m = mod_ref[0]
    h = _modulate(x_ref[...], g_ref[...], m[0:1], m[1:2]).astype(BF16)
    y = _dot(h, win_ref[...])
    c_q = y[:, :MLA_Q_LORA]
    c_kv = _rms(y[:, MLA_Q_LORA:MLA_Q_LORA + MLA_KV_LORA], kvn_ref[...])
    pe_blk = y[:, MLA_Q_LORA + MLA_KV_LORA:]
    ckv_ref[...] = c_kv
    kpe_ref[...] = pe_blk[:, MLA_NOPE:MLA_NOPE + MLA_ROPE]

    lane = lax.broadcasted_iota(jnp.int32, pe_blk.shape, 1)
    if rope:
        table = tab_ref[...]
        k_pe = _rope_lanes(pe_blk, table, jnp.zeros_like(pe_blk))
    else:
        k_pe = jnp.where((lane >= MLA_NOPE) & (lane < MLA_NOPE + MLA_ROPE), pe_blk, 0.0)

    q = _dot(_rms(c_q, qn_ref[...]).astype(BF16), wuq_ref[...])
    c_kv_b = c_kv.astype(BF16)
    kn = _dot(c_kv_b, wuk_ref[...])
    v_ref[...] = (_dot(c_kv_b, wuv_ref[...]) + _value_ones(v_ref.shape[-1])).astype(v_ref.dtype)
    for hd in range(MLA_HEADS):
        lanes = slice(hd * MLA_QK_PAD, (hd + 1) * MLA_QK_PAD)
        qh = q[:, lanes]
        qctx_ref[:, lanes] = qh.astype(qctx_ref.dtype)
        if rope:
            qlat_ref[:, lanes] = _rope_lanes(qh, table, qh).astype(qlat_ref.dtype)
        k_ref[:, lanes] = (kn[:, lanes] + k_pe).astype(k_ref.dtype)


def _proj_mla(x, g, mod, w_in, q_norm, kv_norm, w_uq, w_uk, w_uv, tokens_per_row, rope_table):
    t = x.shape[0]
    tm = min(512, t)
    rope = rope_table is not None
    full = lambda a: pl.BlockSpec(a.shape, lambda i: (0,) * a.ndim)
    tok = lambda n: pl.BlockSpec((tm, n), lambda i: (i, 0))
    in_specs = [tok(D_MODEL), full(g), _mod_spec(tokens_per_row // tm), full(w_in), full(q_norm),
                full(kv_norm), full(w_uq), full(w_uk), full(w_uv)]
    args = [x, g, mod, w_in, q_norm, kv_norm, w_uq, w_uk, w_uv]
    qk = MLA_HEADS * MLA_QK_PAD
    out_specs, out_shape = [], []
    if rope:
        tiles_per_seq = rope_table.shape[0] // tm
        in_specs.append(pl.BlockSpec((tm, LANES), lambda i: (i % tiles_per_seq, 0)))
        args.append(rope_table)
        out_specs.append(tok(qk))
        out_shape.append(jax.ShapeDtypeStruct((t, qk), BF16))
    out_specs += [tok(qk), tok(qk), tok(MLA_HEADS * HEAD_PAIR), tok(MLA_KV_LORA), tok(MLA_ROPE)]
    out_shape += [jax.ShapeDtypeStruct((t, qk), BF16), jax.ShapeDtypeStruct((t, qk), BF16),
                  jax.ShapeDtypeStruct((t, MLA_HEADS * HEAD_PAIR), BF16),
                  jax.ShapeDtypeStruct((t, MLA_KV_LORA), F32), jax.ShapeDtypeStruct((t, MLA_ROPE), F32)]
    return pl.pallas_call(
        functools.partial(_proj_mla_kernel, rope=rope),
        grid=(t // tm,),
        in_specs=in_specs,
        out_specs=out_specs,
        out_shape=out_shape,
        compiler_params=_cparams("parallel"),
        name="proj_mla",
    )(*args)


def _ctx_kv_kernel(ckv_ref, kpe_ref, wuk_ref, wuv_ref, place_ref, k_ref, v_ref):
    c = ckv_ref[...].astype(BF16)
    kn = _dot(c, wuk_ref[...])
    k_pe = _dot(kpe_ref[...].astype(BF16), place_ref[...])
    v_ref[...] = (_dot(c, wuv_ref[...]) + _value_ones(v_ref.shape[-1])).astype(v_ref.dtype)
    for hd in range(MLA_HEADS):
        lanes = slice(hd * MLA_QK_PAD, (hd + 1) * MLA_QK_PAD)
        k_ref[:, lanes] = (kn[:, lanes] + k_pe).astype(k_ref.dtype)


def _ctx_kv(ckv, kpe, w_uk, w_uv, place):
    t = ckv.shape[0]
    tm = min(512, t)
    full = lambda a: pl.BlockSpec(a.shape, lambda i: (0,) * a.ndim)
    tok = lambda n: pl.BlockSpec((tm, n), lambda i: (i, 0))
    return pl.pallas_call(
        _ctx_kv_kernel,
        grid=(t // tm,),
        in_specs=[tok(MLA_KV_LORA), tok(MLA_ROPE), full(w_uk), full(w_uv), full(place)],
        out_specs=[tok(MLA_HEADS * MLA_QK_PAD), tok(MLA_HEADS * HEAD_PAIR)],
        out_shape=[jax.ShapeDtypeStruct((t, MLA_HEADS * MLA_QK_PAD), BF16),
                   jax.ShapeDtypeStruct((t, MLA_HEADS * HEAD_PAIR), BF16)],
        compiler_params=_cparams("parallel"),
        name="mla_ctx_kv",
    )(ckv, kpe, w_uk, w_uv, place)


def _pair_swap(w):
    return w.reshape(w.shape[:-1] + (w.shape[-1] // 2, 2))[..., ::-1].reshape(w.shape)


def _mla_weights(w_in, w_uq, w_uk, w_uv):
    k_pe_cols = w_in[:, MLA_Q_LORA + MLA_KV_LORA:]
    w_in_ext = jnp.concatenate(
        [w_in[:, :MLA_Q_LORA + MLA_KV_LORA], jnp.zeros((D_MODEL, MLA_NOPE), w_in.dtype),
         k_pe_cols, _pair_swap(k_pe_cols)], axis=1).astype(BF16)
    uq = w_uq.reshape(MLA_Q_LORA, MLA_HEADS, MLA_NOPE + MLA_ROPE)
    uq_ext = jnp.concatenate([uq, _pair_swap(uq[..., MLA_NOPE:])], axis=-1)
    uq_ext = uq_ext.reshape(MLA_Q_LORA, MLA_HEADS * MLA_QK_PAD).astype(BF16)
    uk = w_uk.reshape(MLA_KV_LORA, MLA_HEADS, MLA_NOPE)
    uk_ext = jnp.concatenate([uk, jnp.zeros((MLA_KV_LORA, MLA_HEADS, MLA_QK_PAD - MLA_NOPE), uk.dtype)], axis=-1)
    uk_ext = uk_ext.reshape(MLA_KV_LORA, MLA_HEADS * MLA_QK_PAD).astype(BF16)
    uv = w_uv.reshape(MLA_KV_LORA, MLA_HEADS, MLA_V)
    uv_ext = jnp.concatenate([uv, jnp.zeros((MLA_KV_LORA, MLA_HEADS, HEAD_PAIR - MLA_V), uv.dtype)], axis=-1)
    uv_ext = uv_ext.reshape(MLA_KV_LORA, MLA_HEADS * HEAD_PAIR).astype(BF16)
    return w_in_ext, uq_ext, uk_ext, uv_ext


def _rope_table(l):
    t = jnp.arange(l)
    row = (t // GRID_W).astype(F32)
    col = (t % GRID_W).astype(F32)
    nf = MLA_ROPE // 4
    inv = ROPE_BASE ** (-jnp.arange(nf, dtype=F32) / nf)
    ang = jnp.concatenate([row[:, None] * inv, col[:, None] * inv], axis=-1)
    cos, sin = jnp.cos(ang), jnp.sin(ang)
    cc = jnp.repeat(cos, 2, axis=-1)
    ss = jnp.stack([-sin, sin], axis=-1).reshape(l, MLA_ROPE)
    return jnp.concatenate([jnp.ones((l, MLA_NOPE), F32), cc, ss], axis=-1)


def kernel(x_prompt, x_sample, state_ret_fwd, state_ret_bwd, cache_na_k, cache_na_v, cache_mla_ckv,
           cache_mla_kpe, c, c_ctx, w_ada, b_ada, norm_gains, w_mlp_in, w_mlp_out, w_in_ac, w_out_ac,
           ret_decay_fwd, ret_decay_bwd, na_rpb, w_in_c, mla_q_norm, mla_kv_norm, w_uq, w_uk, w_uv, w_out_c):
    bp, lp, _ = x_prompt.shape
    bs, ls, _ = x_sample.shape
    lc = cache_mla_ckv.shape[2]
    xp = x_prompt.reshape(bp * lp, D_MODEL)
    xs = x_sample.reshape(bs * ls, D_MODEL)

    n_cond = bs + 1
    cond = jnp.concatenate([c, c_ctx[None]], axis=0)
    mods = _modulation_all(cond, w_ada, b_ada)
    rope_table = _rope_table(ls)
    place = jnp.zeros((MLA_ROPE, MLA_QK_PAD), BF16).at[
        jnp.arange(MLA_ROPE), MLA_NOPE + jnp.arange(MLA_ROPE)].set(1.0)

    ret_f, ret_b, na_k, na_v, mla_ckv, mla_kpe = [], [], [], [], [], []
    for layer in range(DEPTH):
        mod_s = mods[layer, :bs]
        mod_p = mods[layer, bs:n_cond]
        g = norm_gains[layer]
        g0, g1, g2, g3 = (g[i:i + 1] for i in range(4))
        tp, ts = bp * lp, ls
        if layer % 2 == 0:
            ia = layer // 2
            w_in = w_in_ac[ia].astype(BF16)
            w_out = w_out_ac[ia].astype(BF16)
            lg_f = jnp.log1p(-jnp.exp2(ret_decay_fwd[ia].astype(F32)))
            lg_b = jnp.log1p(-jnp.exp2(ret_decay_bwd[ia].astype(F32)))
            seq = lambda a, b, l: a.reshape(b, l, a.shape[-1])

            rq, rk, rv, rg, nq, nk, nv = [seq(a, bp, lp) for a in _proj_even(xp, g0, mod_p, w_in, tp, F32)]
            o_ret, s_f, s_b = _retention(rq, rk, rv, rg, lg_f, lg_b, None, None, ia, True)
            o_na, k_out, v_out = _attention([nq], [nk], [nv], H_NA, HEAD_DIM, HEAD_DIM ** -0.5, lp,
                                             heads_per_step=H_NA, emit_kv=True)
            ret_f.append(s_f)
            ret_b.append(s_b)
            na_k.append(k_out)
            na_v.append(v_out)
            xp = _outproj_residual([o_ret.reshape(tp, -1), o_na.reshape(tp, -1)], w_out, xp, g1, mod_p, tp, 2)

            rq, rk, rv, rg, nq, nk, nv = [seq(a, bs, ls) for a in _proj_even(xs, g0, mod_s, w_in, ts, BF16)]
            o_ret = _retention(rq, rk, rv, rg, lg_f, lg_b, state_ret_fwd, state_ret_bwd, ia, False)[0]
            o_na = _na_latent(nq, nk, nv, cache_na_k, cache_na_v, ia, _na_bias_tables(na_rpb[ia], ls // GRID_W))
            xs = _outproj_residual([o_ret.reshape(bs * ls, -1), o_na.reshape(bs * ls, -1)], w_out, xs, g1,
                                   mod_s, ts, 2)
        else:
            ic = layer // 2
            w_in_ext, uq_ext, uk_ext, uv = _mla_weights(w_in_c[ic], w_uq[ic], w_uk[ic], w_uv[ic])
            w_out = w_out_c[ic].astype(BF16)
            qn, kvn = mla_q_norm[ic][None], mla_kv_norm[ic][None]

            q, k, v, ckv, kpe = _proj_mla(xp, g0, mod_p, w_in_ext, qn, kvn, uq_ext, uk_ext, uv, tp, None)
            mla_ckv.append(ckv.reshape(bp, lp, MLA_KV_LORA))
            mla_kpe.append(kpe.reshape(bp, lp, MLA_ROPE))
            sq = lambda a, b, l: a.reshape(b, l, a.shape[-1])
            o = _attention([sq(q, bp, lp)], [sq(k, bp, lp)], [sq(v, bp, lp)], MLA_HEADS, MLA_QK_PAD,
                           MLA_SCALE, lp, heads_per_step=8, v_stride=HEAD_PAIR)
            xp = _outproj_residual([o.reshape(tp, -1)], w_out, xp, g1, mod_p, tp, 2)

            q_lat, q_ctx, k, v, _, _ = _proj_mla(xs, g0, mod_s, w_in_ext, qn, kvn, uq_ext, uk_ext, uv, ts,
                                                 rope_table)
            k_c, v_c = _ctx_kv(cache_mla_ckv[:, ic].reshape(bs * lc, MLA_KV_LORA),
                               cache_mla_kpe[:, ic].reshape(bs * lc, MLA_ROPE), uk_ext, uv, place)
            o = _attention_aug([sq(q_lat, bs, ls), sq(q_ctx, bs, ls)], [sq(k, bs, ls), sq(k_c, bs, lc)],
                               [sq(v, bs, ls), sq(v_c, bs, lc)], MLA_HEADS, MLA_QK_PAD, MLA_SCALE, 256,
                               heads_per_step=8)
            xs = _outproj_residual([o.reshape(bs * ls, -1)], w_out, xs, g1, mod_s, ts, 2)

        w1 = w_mlp_in[layer].astype(BF16)
        w2 = w_mlp_out[layer].astype(BF16)
        xp = _mlp_residual(xp, g2, g3, mod_p, w1, w2, tp)
        xs = _mlp_residual(xs, g2, g3, mod_s, w1, w2, ts)

    return (xp.reshape(bp, lp, D_MODEL), xs.reshape(bs, ls, D_MODEL),
            jnp.stack(ret_f, axis=1), jnp.stack(ret_b, axis=1),
            jnp.stack(na_k, axis=1), jnp.stack(na_v, axis=1),
            jnp.stack(mla_ckv, axis=1), jnp.stack(mla_kpe, axis=1))
```

```python
import functools

import numpy as np
import jax
import jax.numpy as jnp
from jax import lax
from jax.experimental import pallas as pl
from jax.experimental.pallas import tpu as pltpu

D_MODEL = 1024
DEPTH = 4
GRID_W = 64
HEAD_DIM = 64
N_HEADS = D_MODEL // HEAD_DIM
H_RET = N_HEADS // 2
H_NA = N_HEADS - H_RET
RET_CHUNK = 128
WIN_R = 8
WIN_C = 16
MLA_HEADS = N_HEADS
MLA_Q_LORA = 384
MLA_KV_LORA = 256
MLA_NOPE = 64
MLA_ROPE = 32
MLA_V = 64
MLA_SCALE = (MLA_NOPE + MLA_ROPE) ** -0.5
ROPE_BASE = 10000.0
D_FF = 4 * D_MODEL
EPS = 1e-6
NEG_INF = -1e30

LANES = 128
HEAD_PAIR = 2 * HEAD_DIM
MLA_QK_PAD = 128
VMEM_LIMIT = 56 * 1024 * 1024

F32 = jnp.float32
BF16 = jnp.bfloat16


def _cparams(*sem):
    return pltpu.CompilerParams(dimension_semantics=sem, vmem_limit_bytes=VMEM_LIMIT)


def _dot(a, b):
    return jnp.dot(a, b, preferred_element_type=F32)


def _dot_nt(a, b):
    return lax.dot_general(a, b, (((1,), (1,)), ((), ())), preferred_element_type=F32)


def _dot_tn(a, b):
    return lax.dot_general(a, b, (((0,), (0,)), ((), ())), preferred_element_type=F32)


def _rms(x, g):
    ms = jnp.mean(x * x, axis=-1, keepdims=True)
    return x * lax.rsqrt(ms + EPS) * g


def _modulate(x, g, shift, scale):
    return _rms(x, g) * (1.0 + scale) + shift


def _mod_kernel(c_ref, w_ref, b_ref, o_ref):
    c = c_ref[...]
    s = c / (1.0 + jnp.exp(-c))
    o_ref[0] = _dot(s.astype(BF16), w_ref[0].astype(BF16)) + b_ref[0]


def _modulation_all(cond, w_ada, b_ada):
    r = cond.shape[0]
    tn = 1536
    out = pl.pallas_call(
        _mod_kernel,
        grid=(DEPTH, 6 * D_MODEL // tn),
        in_specs=[
            pl.BlockSpec((r, D_MODEL), lambda l, j: (0, 0)),
            pl.BlockSpec((1, D_MODEL, tn), lambda l, j: (l, 0, j)),
            pl.BlockSpec((1, 1, tn), lambda l, j: (l, 0, j)),
        ],
        out_specs=pl.BlockSpec((1, r, tn), lambda l, j: (l, 0, j)),
        out_shape=jax.ShapeDtypeStruct((DEPTH, r, 6 * D_MODEL), F32),
        compiler_params=_cparams("parallel", "parallel"),
        name="modulation",
    )(cond, w_ada, b_ada.reshape(DEPTH, 1, 6 * D_MODEL))
    return out.reshape(DEPTH, r, 6, D_MODEL)


def _mod_spec(tiles_per_row):
    return pl.BlockSpec((1, 6, D_MODEL), lambda i: (i // tiles_per_row, 0, 0))


def _proj_even_kernel(x_ref, g_ref, mod_ref, w_ref, *out_refs, mults):
    m = mod_ref[0]
    h = _modulate(x_ref[...], g_ref[...], m[0:1], m[1:2]).astype(BF16)
    width = out_refs[0].shape[-1]
    for i, (o_ref, mult) in enumerate(zip(out_refs, mults)):
        y = _dot(h, w_ref[:, i * width:(i + 1) * width])
        if mult != 1.0:
            y = y * mult
        o_ref[...] = y.astype(o_ref.dtype)


def _proj_even(x, g, mod, w, tokens_per_row, kv_dtype):
    t = x.shape[0]
    tm = min(512, t)
    width = H_RET * HEAD_DIM
    dtypes = [BF16, BF16, BF16, F32, BF16, kv_dtype, kv_dtype]
    mults = (1.0, HEAD_DIM ** -0.5, 1.0, 1.0, 1.0, 1.0, 1.0)
    return pl.pallas_call(
        functools.partial(_proj_even_kernel, mults=mults),
        grid=(t // tm,),
        in_specs=[
            pl.BlockSpec((tm, D_MODEL), lambda i: (i, 0)),
            pl.BlockSpec((1, D_MODEL), lambda i: (0, 0)),
            _mod_spec(tokens_per_row // tm),
            pl.BlockSpec(w.shape, lambda i: (0, 0)),
        ],
        out_specs=[pl.BlockSpec((tm, width), lambda i: (i, 0)) for _ in dtypes],
        out_shape=[jax.ShapeDtypeStruct((t, width), dt) for dt in dtypes],
        compiler_params=_cparams("parallel"),
        name="proj_even",
    )(x, g, mod, w)


def _retention_kernel(lgf_ref, lgb_ref, q_ref, k_ref, v_ref, g_ref, *rest, n_chunks, has_state, emit_state):
    if has_state:
        sf0_ref, sb0_ref = rest[0], rest[1]
        rest = rest[2:]
    o_ref = rest[0]
    rest = rest[1:]
    if emit_state:
        sfo_ref, sbo_ref = rest[0], rest[1]
        rest = rest[2:]
    acc_ref, kv_ref, st_ref, tab_ref = rest

    c = RET_CHUNK
    d = HEAD_DIM
    pair = pl.program_id(0)
    first_half = lax.broadcasted_iota(jnp.int32, (c, HEAD_PAIR), 1) < d

    @pl.when(pl.program_id(1) == 0)
    def _():
        row = lax.broadcasted_iota(jnp.int32, (c, c), 0).astype(F32)
        col = lax.broadcasted_iota(jnp.int32, (c, c), 1).astype(F32)
        diff = row - col
        pos = lax.broadcasted_iota(jnp.int32, (c, HEAD_PAIR), 0).astype(F32)
        for j in range(2):
            lgf = lgf_ref[2 * pair + j]
            lgb = lgb_ref[2 * pair + j]
            tab_ref[j, 0] = (jnp.where(diff >= 0, jnp.exp(lgf * jnp.maximum(diff, 0.0)), 0.0)
                             + jnp.where(diff <= 0, jnp.exp(lgb * jnp.maximum(-diff, 0.0)), 0.0))
            tab_ref[j, 1] = jnp.where(first_half, jnp.exp(lgf * (c - 1 - pos)), jnp.exp(lgb * pos))
            tab_ref[j, 2] = jnp.where(first_half, jnp.exp(lgf * (pos + 1.0)), jnp.exp(lgb * (c - pos)))

    cdecs = []
    for j in range(2):
        lgf = lgf_ref[2 * pair + j]
        lgb = lgb_ref[2 * pair + j]
        cdecs.append((jnp.exp(jnp.full((1, 1), lgf * c, F32)), jnp.exp(jnp.full((1, 1), lgb * c, F32))))

    def doubled(tile):
        t = tile.astype(F32)
        r = pltpu.roll(t, d, 1)
        return jnp.where(first_half, t, r), jnp.where(first_half, r, t)

    def chunk_rows(n):
        return pl.ds(pl.multiple_of(n * c, c), c)

    halves = (first_half, jnp.logical_not(first_half))
    same_head = ((lax.broadcasted_iota(jnp.int32, (HEAD_PAIR, HEAD_PAIR), 0) < d)
                 == (lax.broadcasted_iota(jnp.int32, (HEAD_PAIR, HEAD_PAIR), 1) < d))
    head_mean = jnp.where(same_head, 1.0 / d, 0.0).astype(BF16)

    def group_mean(x):
        hi = x.astype(BF16)
        lo = (x - hi.astype(F32)).astype(BF16)
        return _dot(hi, head_mean) + _dot(lo, head_mean)

    def intra(n, carry):
        rows = chunk_rows(n)
        q, k, v = q_ref[0, rows, :], k_ref[0, rows, :], v_ref[0, rows, :]
        kf = k.astype(F32)
        k2 = doubled(k)
        o = None
        for j in range(2):
            kj = jnp.where(halves[j], kf, 0.0).astype(BF16)
            s = (_dot_nt(q, kj) * tab_ref[j, 0]).astype(BF16)
            oj = _dot(s, v)
            o = oj if o is None else jnp.where(first_half, o, oj)
            kv = _dot_tn((k2[j] * tab_ref[j, 1]).astype(BF16), v)
            kv_ref[n, j] = jnp.where(halves[j], kv, 0.0)
        acc_ref[rows, :] = o
        return carry

    lax.fori_loop(0, n_chunks, intra, 0, unroll=min(4, n_chunks))

    for j in range(2):
        zeros = jnp.zeros((d, d), F32)
        if has_state:
            place = (lambda s: jnp.concatenate([s, zeros], axis=1)) if j == 0 else (
                lambda s: jnp.concatenate([zeros, s], axis=1))
            sf0, sb0 = place(sf0_ref[0, 0, j]), place(sb0_ref[0, 0, j])
        else:
            sf0 = sb0 = jnp.zeros((d, HEAD_PAIR), F32)

        def scan_f(n, state):
            st_ref[n, 2 * j * d:(2 * j + 1) * d, :] = state.astype(BF16)
            return cdecs[j][0] * state + kv_ref[n, j, 0:d, :]

        def scan_b(i, state):
            n = n_chunks - 1 - i
            st_ref[n, (2 * j + 1) * d:(2 * j + 2) * d, :] = state.astype(BF16)
            return cdecs[j][1] * state + kv_ref[n, j, d:2 * d, :]

        fin_f = lax.fori_loop(0, n_chunks, scan_f, sf0)
        fin_b = lax.fori_loop(0, n_chunks, scan_b, sb0)
        if emit_state:
            sfo_ref[0, j] = fin_f[:, j * d:(j + 1) * d]
            sbo_ref[0, j] = fin_b[:, j * d:(j + 1) * d]

    def finish(n, carry):
        rows = chunk_rows(n)
        q2 = doubled(q_ref[0, rows, :])
        qd = jnp.concatenate([(q2[j] * tab_ref[j, 2]).astype(BF16) for j in range(2)], axis=1)
        o = acc_ref[rows, :] + _dot(qd, st_ref[n])
        cen = o - group_mean(o)
        on = cen * lax.rsqrt(group_mean(cen * cen) + EPS)
        gate = g_ref[0, rows, :]
        o_ref[0, rows, :] = (on * (gate / (1.0 + jnp.exp(-gate)))).astype(o_ref.dtype)
        return carry

    lax.fori_loop(0, n_chunks, finish, 0, unroll=min(4, n_chunks))


def _retention(q, k, v, g, lg_f, lg_b, state_f, state_b, ia, emit_state):
    b, l, _ = q.shape
    has_state = state_f is not None
    blk = lambda: pl.BlockSpec((1, l, HEAD_PAIR), lambda p, bi: (bi, 0, p))
    smem = pl.BlockSpec(memory_space=pltpu.SMEM)
    in_specs = [smem, smem, blk(), blk(), blk(), blk()]
    args = [lg_f, lg_b, q, k, v, g]
    if has_state:
        st = lambda: pl.BlockSpec((1, 1, 2, HEAD_DIM, HEAD_DIM), lambda p, bi: (bi, ia, p, 0, 0))
        in_specs += [st(), st()]
        args += [state_f, state_b]
    out_specs = [blk()]
    out_shape = [jax.ShapeDtypeStruct((b, l, H_RET * HEAD_DIM), BF16)]
    if emit_state:
        so = lambda: pl.BlockSpec((1, 2, HEAD_DIM, HEAD_DIM), lambda p, bi: (bi, p, 0, 0))
        out_specs += [so(), so()]
        out_shape += [jax.ShapeDtypeStruct((b, H_RET, HEAD_DIM, HEAD_DIM), F32)] * 2
    return pl.pallas_call(
        functools.partial(_retention_kernel, n_chunks=l // RET_CHUNK, has_state=has_state,
                          emit_state=emit_state),
        grid=(H_RET // 2, b),
        in_specs=in_specs,
        out_specs=out_specs,
        out_shape=out_shape,
        scratch_shapes=[pltpu.VMEM((l, HEAD_PAIR), F32),
                        pltpu.VMEM((l // RET_CHUNK, 2, HEAD_PAIR, HEAD_PAIR), F32),
                        pltpu.VMEM((l // RET_CHUNK, 2 * HEAD_PAIR, HEAD_PAIR), BF16),
                        pltpu.VMEM((2, 3, RET_CHUNK, HEAD_PAIR), F32)],
        compiler_params=_cparams("arbitrary", "arbitrary"),
        name="retention",
    )(*args)


LOG2_E = 1.4426950408889634


def _attn_aug_kernel(*refs, nseg, dq, scale, heads):
    q_refs = refs[0:nseg]
    k_refs = refs[nseg:2 * nseg]
    v_refs = refs[2 * nseg:3 * nseg]
    o_ref = refs[3 * nseg]
    first_half = lax.broadcasted_iota(jnp.int32, (o_ref.shape[1], HEAD_PAIR), 1) < HEAD_DIM
    for pair in range(heads // 2):
        res = []
        for j in range(2 * pair, 2 * pair + 2):
            ql = slice(j * dq, (j + 1) * dq)
            vl = slice(j * HEAD_PAIR, (j + 1) * HEAD_PAIR)
            scores = [_dot_nt(q_refs[s][0, :, ql], k_refs[s][0, :, ql]) for s in range(nseg)]
            m = functools.reduce(jnp.maximum, [jnp.max(s, axis=-1, keepdims=True) for s in scores])
            acc = None
            for s in range(nseg):
                p = jnp.exp2((scores[s] - m) * (scale * LOG2_E)).astype(BF16)
                pv = _dot(p, v_refs[s][0, :, vl])
                acc = pv if acc is None else acc + pv
            res.append(acc / pltpu.roll(acc, HEAD_DIM, 1))
        out = jnp.where(first_half, res[0], pltpu.roll(res[1], HEAD_DIM, 1))
        o_ref[0, :, pair * HEAD_PAIR:(pair + 1) * HEAD_PAIR] = out.astype(o_ref.dtype)


def _attention_aug(qs, ks, vs, n_heads, dq, scale, tq, heads_per_step=2):
    nseg = len(qs)
    b, lq, _ = qs[0].shape
    hs = heads_per_step
    in_specs = [pl.BlockSpec((1, tq, hs * dq), lambda bi, p, qi: (bi, qi, p)) for _ in qs]
    in_specs += [pl.BlockSpec((1, k.shape[1], hs * dq), lambda bi, p, qi: (bi, 0, p)) for k in ks]
    in_specs += [pl.BlockSpec((1, v.shape[1], hs * HEAD_PAIR), lambda bi, p, qi: (bi, 0, p)) for v in vs]
    return pl.pallas_call(
        functools.partial(_attn_aug_kernel, nseg=nseg, dq=dq, scale=scale, heads=hs),
        grid=(b, n_heads // hs, lq // tq),
        in_specs=in_specs,
        out_specs=pl.BlockSpec((1, tq, hs * HEAD_DIM), lambda bi, p, qi: (bi, qi, p)),
        out_shape=jax.ShapeDtypeStruct((b, lq, n_heads * HEAD_DIM), BF16),
        compiler_params=_cparams("parallel", "parallel", "arbitrary"),
        name="attention_mla",
    )(*qs, *ks, *vs)


def _attn_kernel(*refs, nseg, dq, scale, heads, emit_kv, v_stride):
    q_refs = refs[0:nseg]
    k_refs = refs[nseg:2 * nseg]
    v_refs = refs[2 * nseg:3 * nseg]
    o_ref = refs[3 * nseg]
    for pair in range(heads // 2):
        outs = []
        for j in range(2 * pair, 2 * pair + 2):
            ql = slice(j * dq, (j + 1) * dq)
            vl = slice(j * v_stride, j * v_stride + HEAD_DIM)
            scores = [_dot_nt(q_refs[s][0, :, ql], k_refs[s][0, :, ql].astype(BF16)) for s in range(nseg)]
            m = functools.reduce(jnp.maximum, [jnp.max(s, axis=-1, keepdims=True) for s in scores])
            acc = None
            den = None
            for s in range(nseg):
                p = jnp.exp((scores[s] - m) * scale)
                ps = jnp.sum(p, axis=-1, keepdims=True)
                pv = _dot(p.astype(BF16), v_refs[s][0, :, vl].astype(BF16))
                acc = pv if acc is None else acc + pv
                den = ps if den is None else den + ps
            outs.append(acc / den)
        o_ref[0, :, pair * HEAD_PAIR:(pair + 1) * HEAD_PAIR] = jnp.concatenate(outs, axis=-1).astype(o_ref.dtype)
    if emit_kv:
        ko_ref, vo_ref = refs[3 * nseg + 1], refs[3 * nseg + 2]
        for j in range(heads):
            ko_ref[0, j] = k_refs[0][0, :, j * dq:(j + 1) * dq]
            vo_ref[0, j] = v_refs[0][0, :, j * HEAD_DIM:(j + 1) * HEAD_DIM]


def _attention(qs, ks, vs, n_heads, dq, scale, tq, heads_per_step=2, emit_kv=False, v_stride=HEAD_DIM):
    nseg = len(qs)
    b, lq, _ = qs[0].shape
    hs = heads_per_step
    in_specs = [pl.BlockSpec((1, tq, hs * dq), lambda bi, p, qi: (bi, qi, p)) for _ in qs]
    in_specs += [pl.BlockSpec((1, k.shape[1], hs * dq), lambda bi, p, qi: (bi, 0, p)) for k in ks]
    in_specs += [pl.BlockSpec((1, v.shape[1], hs * v_stride), lambda bi, p, qi: (bi, 0, p)) for v in vs]
    out_specs = [pl.BlockSpec((1, tq, hs * HEAD_DIM), lambda bi, p, qi: (bi, qi, p))]
    out_shape = [jax.ShapeDtypeStruct((b, lq, n_heads * HEAD_DIM), BF16)]
    if emit_kv:
        assert nseg == 1 and tq == lq and dq == HEAD_DIM
        lk = ks[0].shape[1]
        kv = lambda: pl.BlockSpec((1, hs, lk, HEAD_DIM), lambda bi, p, qi: (bi, p, 0, 0))
        out_specs += [kv(), kv()]
        out_shape += [jax.ShapeDtypeStruct((b, n_heads, lk, HEAD_DIM), F32)] * 2
    res = pl.pallas_call(
        functools.partial(_attn_kernel, nseg=nseg, dq=dq, scale=scale, heads=hs, emit_kv=emit_kv,
                          v_stride=v_stride),
        grid=(b, n_heads // hs, lq // tq),
        in_specs=in_specs,
        out_specs=out_specs,
        out_shape=out_shape,
        compiler_params=_cparams("parallel", "parallel", "arbitrary"),
        name="attention",
    )(*qs, *ks, *vs)
    return res if emit_kv else res[0]


NA_QROWS = 4
NA_KROWS = NA_QROWS + WIN_R - 1


def _na_block_start(g, rows):
    return np.clip(NA_QROWS * g - WIN_R // 2, 0, rows - NA_KROWS)


def _na_bias_tables(rpb, rows):
    h = rpb.shape[0]
    w = GRID_W
    nd = 2 * WIN_R - 1
    period = 2 * w - 1
    lo = w - WIN_C
    u = jnp.pad(rpb.astype(F32), ((0, 0), (0, 0), (lo, period - lo - (2 * WIN_C - 1))))
    big = jnp.tile(u, (1, 1, w + 1))[..., :w * 2 * w].reshape(h, nd, w, 2 * w)
    colb = big[:, :, ::-1, :w]
    col = np.arange(w)
    win_start = np.clip(col - WIN_C // 2, 0, w - WIN_C)
    valid = (col[None, :] >= win_start[:, None]) & (col[None, :] < win_start[:, None] + WIN_C)
    colb = jnp.where(valid[None, None], colb, NEG_INF)
    neg = jnp.full((h, w, w), NEG_INF, F32)
    nblk = rows // NA_QROWS
    tables = []
    for g in (0, 1, nblk - 1):
        ks = _na_block_start(g, rows)
        blk_rows = []
        for rr in range(NA_QROWS):
            r = NA_QROWS * g + rr
            rs = np.clip(r - WIN_R // 2, 0, rows - WIN_R)
            pieces = []
            for jj in range(NA_KROWS):
                krow = ks + jj
                pieces.append(colb[:, krow - r + WIN_R - 1] if rs <= krow < rs + WIN_R else neg)
            blk_rows.append(jnp.concatenate(pieces, axis=-1))
        tables.append(jnp.concatenate(blk_rows, axis=1))
    return jnp.stack(tables, axis=1)


def _na_latent_kernel(q_ref, k_ref, v_ref, kc_ref, vc_ref, bias_ref, o_ref, *, rows, heads):
    scale = HEAD_DIM ** -0.5
    nq = NA_QROWS * GRID_W
    nk = NA_KROWS * GRID_W
    nblk = rows // NA_QROWS

    def body(g, carry):
        ks = jnp.clip(NA_QROWS * g - WIN_R // 2, 0, rows - NA_KROWS)
        case = jnp.where(g == 0, 0, jnp.where(g == nblk - 1, 2, 1))
        qrows = pl.ds(pl.multiple_of(g * nq, nq), nq)
        krows = pl.ds(pl.multiple_of(ks * GRID_W, GRID_W), nk)
        outs = []
        for j in range(heads):
            lanes = slice(j * HEAD_DIM, (j + 1) * HEAD_DIM)
            q = q_ref[0, qrows, lanes]
            s_loc = _dot_nt(q, k_ref[0, krows, lanes]) * scale + bias_ref[j, case]
            s_ctx = _dot_nt(q, kc_ref[0, 0, j].astype(BF16)) * scale
            m = jnp.maximum(jnp.max(s_loc, axis=-1, keepdims=True), jnp.max(s_ctx, axis=-1, keepdims=True))
            p_loc = jnp.exp(s_loc - m)
            p_ctx = jnp.exp(s_ctx - m)
            den = jnp.sum(p_loc, axis=-1, keepdims=True) + jnp.sum(p_ctx, axis=-1, keepdims=True)
            acc = (_dot(p_loc.astype(BF16), v_ref[0, krows, lanes])
                   + _dot(p_ctx.astype(BF16), vc_ref[0, 0, j].astype(BF16)))
            outs.append(acc / den)
        o_ref[0, qrows, :] = jnp.concatenate(outs, axis=-1).astype(o_ref.dtype)
        return carry

    lax.fori_loop(0, nblk, body, 0)


def _na_latent(q, k, v, cache_k, cache_v, ia, bias):
    b, l, _ = q.shape
    lc = cache_k.shape[3]
    rows = l // GRID_W
    assert rows % NA_QROWS == 0 and rows >= NA_KROWS + NA_QROWS
    hs = 4
    blk = lambda: pl.BlockSpec((1, l, hs * HEAD_DIM), lambda p, bi: (bi, 0, p))
    ctx = lambda: pl.BlockSpec((1, 1, hs, lc, HEAD_DIM), lambda p, bi: (bi, ia, p, 0, 0))
    return pl.pallas_call(
        functools.partial(_na_latent_kernel, rows=rows, heads=hs),
        grid=(H_NA // hs, b),
        in_specs=[blk(), blk(), blk(), ctx(), ctx(),
                  pl.BlockSpec((hs,) + bias.shape[1:], lambda p, bi: (p, 0, 0, 0))],
        out_specs=blk(),
        out_shape=jax.ShapeDtypeStruct((b, l, H_NA * HEAD_DIM), BF16),
        compiler_params=_cparams("parallel", "parallel"),
        name="na_latent",
    )(q, k, v, cache_k, cache_v, bias)


def _outproj_kernel(*refs, n_in, gate_i):
    a_refs = refs[:n_in]
    w_ref, x_ref, g_ref, mod_ref, o_ref = refs[n_in:]
    y = None
    k0 = 0
    for a_ref in a_refs:
        kw = a_ref.shape[-1]
        part = _dot(a_ref[...], w_ref[k0:k0 + kw, :])
        y = part if y is None else y + part
        k0 += kw
    m = mod_ref[0]
    o_ref[...] = x_ref[...] + m[gate_i:gate_i + 1] * _rms(y, g_ref[...])


def _outproj_residual(acts, w, x, g, mod, tokens_per_row, gate_i):
    t = x.shape[0]
    tm = min(512, t)
    in_specs = [pl.BlockSpec((tm, a.shape[1]), lambda i: (i, 0)) for a in acts]
    in_specs += [
        pl.BlockSpec(w.shape, lambda i: (0, 0)),
        pl.BlockSpec((tm, D_MODEL), lambda i: (i, 0)),
        pl.BlockSpec((1, D_MODEL), lambda i: (0, 0)),
        _mod_spec(tokens_per_row // tm),
    ]
    return pl.pallas_call(
        functools.partial(_outproj_kernel, n_in=len(acts), gate_i=gate_i),
        grid=(t // tm,),
        in_specs=in_specs,
        out_specs=pl.BlockSpec((tm, D_MODEL), lambda i: (i, 0)),
        out_shape=jax.ShapeDtypeStruct((t, D_MODEL), F32),
        compiler_params=_cparams("parallel"),
        name="outproj_residual",
    )(*acts, w, x, g, mod)


def _mlp_kernel(x_ref, gin_ref, gout_ref, mod_ref, w1_ref, w2_ref, o_ref, h_ref, acc_ref):
    f = pl.program_id(1)

    @pl.when(f == 0)
    def _():
        m = mod_ref[0]
        h_ref[...] = _modulate(x_ref[...], gin_ref[...], m[3:4], m[4:5]).astype(BF16)

    a = jnp.maximum(_dot(h_ref[...], w1_ref[...]), 0.0)
    part = _dot((a * a).astype(BF16), w2_ref[...])

    @pl.when(f == 0)
    def _():
        acc_ref[...] = part

    @pl.when(f > 0)
    def _():
        acc_ref[...] += part

    @pl.when(f == pl.num_programs(1) - 1)
    def _():
        m = mod_ref[0]
        o_ref[...] = x_ref[...] + m[5:6] * _rms(acc_ref[...], gout_ref[...])


def _mlp_residual(x, g_in, g_out, mod, w1, w2, tokens_per_row):
    t = x.shape[0]
    tm = min(1024, t)
    tf = 2048
    return pl.pallas_call(
        _mlp_kernel,
        grid=(t // tm, D_FF // tf),
        in_specs=[
            pl.BlockSpec((tm, D_MODEL), lambda i, f: (i, 0)),
            pl.BlockSpec((1, D_MODEL), lambda i, f: (0, 0)),
            pl.BlockSpec((1, D_MODEL), lambda i, f: (0, 0)),
            pl.BlockSpec((1, 6, D_MODEL), lambda i, f: (i // (tokens_per_row // tm), 0, 0)),
            pl.BlockSpec((D_MODEL, tf), lambda i, f: (0, f)),
            pl.BlockSpec((tf, D_MODEL), lambda i, f: (f, 0)),
        ],
        out_specs=pl.BlockSpec((tm, D_MODEL), lambda i, f: (i, 0)),
        out_shape=jax.ShapeDtypeStruct((t, D_MODEL), F32),
        scratch_shapes=[pltpu.VMEM((tm, D_MODEL), BF16), pltpu.VMEM((tm, D_MODEL), F32)],
        compiler_params=_cparams("parallel", "arbitrary"),
        name="mlp_residual",
    )(x, g_in, g_out, mod, w1, w2)


def _value_ones(n):
    lane = lax.broadcasted_iota(jnp.int32, (1, n), 1)
    return jnp.where((lane & HEAD_DIM) != 0, 1.0, 0.0)


def _rope_lanes(y, table, keep):
    lane = lax.broadcasted_iota(jnp.int32, y.shape, 1)
    prod = y * table
    rot = prod + pltpu.roll(prod, LANES - MLA_ROPE, 1)
    return jnp.where((lane >= MLA_NOPE) & (lane < MLA_NOPE + MLA_ROPE), rot, keep)


def _proj_mla_kernel(*refs, rope):
    (x_ref, g_ref, mod_ref, win_ref, qn_ref, kvn_ref, wuq_ref, wuk_ref, wuv_ref) = refs[:9]
    refs = refs[9:]
    if rope:
        tab_ref, refs = refs[0], refs[1:]
        qlat_ref, refs = refs[0], refs[1:]
    qctx_ref, k_ref, v_ref, ckv_ref, kpe_ref = refs

    m = mod_ref[0]
    h = _modulate(x_ref[...], g_ref[...], m[0:1], m[1:2]).astype(BF16)
    y = _dot(h, win_ref[...])
    c_q = y[:, :MLA_Q_LORA]
    c_kv = _rms(y[:, MLA_Q_LORA:MLA_Q_LORA + MLA_KV_LORA], kvn_ref[...])
    pe_blk = y[:, MLA_Q_LORA + MLA_KV_LORA:]
    ckv_ref[...] = c_kv
    kpe_ref[...] = pe_blk[:, MLA_NOPE:MLA_NOPE + MLA_ROPE]

    lane = lax.broadcasted_iota(jnp.int32, pe_blk.shape, 1)
    if rope:
        table = tab_ref[...]
        k_pe = _rope_lanes(pe_blk, table, jnp.zeros_like(pe_blk))
    else:
        k_pe = jnp.where((lane >= MLA_NOPE) & (lane < MLA_NOPE + MLA_ROPE), pe_blk, 0.0)

    q = _dot(_rms(c_q, qn_ref[...]).astype(BF16), wuq_ref[...])
    c_kv_b = c_kv.astype(BF16)
    kn = _dot(c_kv_b, wuk_ref[...])
    v_ref[...] = (_dot(c_kv_b, wuv_ref[...]) + _value_ones(v_ref.shape[-1])).astype(v_ref.dtype)
    for hd in range(MLA_HEADS):
        lanes = slice(hd * MLA_QK_PAD, (hd + 1) * MLA_QK_PAD)
        qh = q[:, lanes]
        qctx_ref[:, lanes] = qh.astype(qctx_ref.dtype)
        if rope:
            qlat_ref[:, lanes] = _rope_lanes(qh, table, qh).astype(qlat_ref.dtype)
        k_ref[:, lanes] = (kn[:, lanes] + k_pe).astype(k_ref.dtype)


def _proj_mla(x, g, mod, w_in, q_norm, kv_norm, w_uq, w_uk, w_uv, tokens_per_row, rope_table):
    t = x.shape[0]
    tm = min(512, t)
    rope = rope_table is not None
    full = lambda a: pl.BlockSpec(a.shape, lambda i: (0,) * a.ndim)
    tok = lambda n: pl.BlockSpec((tm, n), lambda i: (i, 0))
    in_specs = [tok(D_MODEL), full(g), _mod_spec(tokens_per_row // tm), full(w_in), full(q_norm),
                full(kv_norm), full(w_uq), full(w_uk), full(w_uv)]
    args = [x, g, mod, w_in, q_norm, kv_norm, w_uq, w_uk, w_uv]
    qk = MLA_HEADS * MLA_QK_PAD
    out_specs, out_shape = [], []
    if rope:
        tiles_per_seq = rope_table.shape[0] // tm
        in_specs.append(pl.BlockSpec((tm, LANES), lambda i: (i % tiles_per_seq, 0)))
        args.append(rope_table)
        out_specs.append(tok(qk))
        out_shape.append(jax.ShapeDtypeStruct((t, qk), BF16))
    out_specs += [tok(qk), tok(qk), tok(MLA_HEADS * HEAD_PAIR), tok(MLA_KV_LORA), tok(MLA_ROPE)]
    out_shape += [jax.ShapeDtypeStruct((t, qk), BF16), jax.ShapeDtypeStruct((t, qk), BF16),
                  jax.ShapeDtypeStruct((t, MLA_HEADS * HEAD_PAIR), BF16),
                  jax.ShapeDtypeStruct((t, MLA_KV_LORA), F32), jax.ShapeDtypeStruct((t, MLA_ROPE), F32)]
    return pl.pallas_call(
        functools.partial(_proj_mla_kernel, rope=rope),
        grid=(t // tm,),
        in_specs=in_specs,
        out_specs=out_specs,
        out_shape=out_shape,
        compiler_params=_cparams("parallel"),
        name="proj_mla",
    )(*args)


def _ctx_kv_kernel(ckv_ref, kpe_ref, wuk_ref, wuv_ref, place_ref, k_ref, v_ref):
    c = ckv_ref[...].astype(BF16)
    kn = _dot(c, wuk_ref[...])
    k_pe = _dot(kpe_ref[...].astype(BF16), place_ref[...])
    v_ref[...] = (_dot(c, wuv_ref[...]) + _value_ones(v_ref.shape[-1])).astype(v_ref.dtype)
    for hd in range(MLA_HEADS):
        lanes = slice(hd * MLA_QK_PAD, (hd + 1) * MLA_QK_PAD)
        k_ref[:, lanes] = (kn[:, lanes] + k_pe).astype(k_ref.dtype)


def _ctx_kv(ckv, kpe, w_uk, w_uv, place):
    t = ckv.shape[0]
    tm = min(512, t)
    full = lambda a: pl.BlockSpec(a.shape, lambda i: (0,) * a.ndim)
    tok = lambda n: pl.BlockSpec((tm, n), lambda i: (i, 0))
    return pl.pallas_call(
        _ctx_kv_kernel,
        grid=(t // tm,),
        in_specs=[tok(MLA_KV_LORA), tok(MLA_ROPE), full(w_uk), full(w_uv), full(place)],
        out_specs=[tok(MLA_HEADS * MLA_QK_PAD), tok(MLA_HEADS * HEAD_PAIR)],
        out_shape=[jax.ShapeDtypeStruct((t, MLA_HEADS * MLA_QK_PAD), BF16),
                   jax.ShapeDtypeStruct((t, MLA_HEADS * HEAD_PAIR), BF16)],
        compiler_params=_cparams("parallel"),
        name="mla_ctx_kv",
    )(ckv, kpe, w_uk, w_uv, place)


def _pair_swap(w):
    return w.reshape(w.shape[:-1] + (w.shape[-1] // 2, 2))[..., ::-1].reshape(w.shape)


def _mla_weights(w_in, w_uq, w_uk, w_uv):
    k_pe_cols = w_in[:, MLA_Q_LORA + MLA_KV_LORA:]
    w_in_ext = jnp.concatenate(
        [w_in[:, :MLA_Q_LORA + MLA_KV_LORA], jnp.zeros((D_MODEL, MLA_NOPE), w_in.dtype),
         k_pe_cols, _pair_swap(k_pe_cols)], axis=1).astype(BF16)
    uq = w_uq.reshape(MLA_Q_LORA, MLA_HEADS, MLA_NOPE + MLA_ROPE)
    uq_ext = jnp.concatenate([uq, _pair_swap(uq[..., MLA_NOPE:])], axis=-1)
    uq_ext = uq_ext.reshape(MLA_Q_LORA, MLA_HEADS * MLA_QK_PAD).astype(BF16)
    uk = w_uk.reshape(MLA_KV_LORA, MLA_HEADS, MLA_NOPE)
    uk_ext = jnp.concatenate([uk, jnp.zeros((MLA_KV_LORA, MLA_HEADS, MLA_QK_PAD - MLA_NOPE), uk.dtype)], axis=-1)
    uk_ext = uk_ext.reshape(MLA_KV_LORA, MLA_HEADS * MLA_QK_PAD).astype(BF16)
    uv = w_uv.reshape(MLA_KV_LORA, MLA_HEADS, MLA_V)
    uv_ext = jnp.concatenate([uv, jnp.zeros((MLA_KV_LORA, MLA_HEADS, HEAD_PAIR - MLA_V), uv.dtype)], axis=-1)
    uv_ext = uv_ext.reshape(MLA_KV_LORA, MLA_HEADS * HEAD_PAIR).astype(BF16)
    return w_in_ext, uq_ext, uk_ext, uv_ext


def _rope_table(l):
    t = jnp.arange(l)
    row = (t // GRID_W).astype(F32)
    col = (t % GRID_W).astype(F32)
    nf = MLA_ROPE // 4
    inv = ROPE_BASE ** (-jnp.arange(nf, dtype=F32) / nf)
    ang = jnp.concatenate([row[:, None] * inv, col[:, None] * inv], axis=-1)
    cos, sin = jnp.cos(ang), jnp.sin(ang)
    cc = jnp.repeat(cos, 2, axis=-1)
    ss = jnp.stack([-sin, sin], axis=-1).reshape(l, MLA_ROPE)
    return jnp.concatenate([jnp.ones((l, MLA_NOPE), F32), cc, ss], axis=-1)


def kernel(x_prompt, x_sample, state_ret_fwd, state_ret_bwd, cache_na_k, cache_na_v, cache_mla_ckv,
           cache_mla_kpe, c, c_ctx, w_ada, b_ada, norm_gains, w_mlp_in, w_mlp_out, w_in_ac, w_out_ac,
           ret_decay_fwd, ret_decay_bwd, na_rpb, w_in_c, mla_q_norm, mla_kv_norm, w_uq, w_uk, w_uv, w_out_c):
    bp, lp, _ = x_prompt.shape
    bs, ls, _ = x_sample.shape
    lc = cache_mla_ckv.shape[2]
    xp = x_prompt.reshape(bp * lp, D_MODEL)
    xs = x_sample.reshape(bs * ls, D_MODEL)

    n_cond = bs + 1
    cond = jnp.concatenate([c, c_ctx[None]], axis=0)
    mods = _modulation_all(cond, w_ada, b_ada)
    rope_table = _rope_table(ls)
    place = jnp.zeros((MLA_ROPE, MLA_QK_PAD), BF16).at[
        jnp.arange(MLA_ROPE), MLA_NOPE + jnp.arange(MLA_ROPE)].set(1.0)

    ret_f, ret_b, na_k, na_v, mla_ckv, mla_kpe = [], [], [], [], [], []
    for layer in range(DEPTH):
        mod_s = mods[layer, :bs]
        mod_p = mods[layer, bs:n_cond]
        g = norm_gains[layer]
        g0, g1, g2, g3 = (g[i:i + 1] for i in range(4))
        tp, ts = bp * lp, ls
        if layer % 2 == 0:
            ia = layer // 2
            w_in = w_in_ac[ia].astype(BF16)
            w_out = w_out_ac[ia].astype(BF16)
            lg_f = jnp.log1p(-jnp.exp2(ret_decay_fwd[ia].astype(F32)))
            lg_b = jnp.log1p(-jnp.exp2(ret_decay_bwd[ia].astype(F32)))
            seq = lambda a, b, l: a.reshape(b, l, a.shape[-1])

            rq, rk, rv, rg, nq, nk, nv = [seq(a, bp, lp) for a in _proj_even(xp, g0, mod_p, w_in, tp, F32)]
            o_ret, s_f, s_b = _retention(rq, rk, rv, rg, lg_f, lg_b, None, None, ia, True)
            o_na, k_out, v_out = _attention([nq], [nk], [nv], H_NA, HEAD_DIM, HEAD_DIM ** -0.5, lp,
                                             heads_per_step=H_NA, emit_kv=True)
            ret_f.append(s_f)
            ret_b.append(s_b)
            na_k.append(k_out)
            na_v.append(v_out)
            xp = _outproj_residual([o_ret.reshape(tp, -1), o_na.reshape(tp, -1)], w_out, xp, g1, mod_p, tp, 2)

            rq, rk, rv, rg, nq, nk, nv = [seq(a, bs, ls) for a in _proj_even(xs, g0, mod_s, w_in, ts, BF16)]
            o_ret = _retention(rq, rk, rv, rg, lg_f, lg_b, state_ret_fwd, state_ret_bwd, ia, False)[0]
            o_na = _na_latent(nq, nk, nv, cache_na_k, cache_na_v, ia, _na_bias_tables(na_rpb[ia], ls // GRID_W))
            xs = _outproj_residual([o_ret.reshape(bs * ls, -1), o_na.reshape(bs * ls, -1)], w_out, xs, g1,
                                   mod_s, ts, 2)
        else:
            ic = layer // 2
            w_in_ext, uq_ext, uk_ext, uv = _mla_weights(w_in_c[ic], w_uq[ic], w_uk[ic], w_uv[ic])
            w_out = w_out_c[ic].astype(BF16)
            qn, kvn = mla_q_norm[ic][None], mla_kv_norm[ic][None]

            q, k, v, ckv, kpe = _proj_mla(xp, g0, mod_p, w_in_ext, qn, kvn, uq_ext, uk_ext, uv, tp, None)
            mla_ckv.append(ckv.reshape(bp, lp, MLA_KV_LORA))
            mla_kpe.append(kpe.reshape(bp, lp, MLA_ROPE))
            sq = lambda a, b, l: a.reshape(b, l, a.shape[-1])
            o = _attention([sq(q, bp, lp)], [sq(k, bp, lp)], [sq(v, bp, lp)], MLA_HEADS, MLA_QK_PAD,
                           MLA_SCALE, lp, heads_per_step=8, v_stride=HEAD_PAIR)
            xp = _outproj_residual([o.reshape(tp, -1)], w_out, xp, g1, mod_p, tp, 2)

            q_lat, q_ctx, k, v, _, _ = _proj_mla(xs, g0, mod_s, w_in_ext, qn, kvn, uq_ext, uk_ext, uv, ts,
                                                 rope_table)
            k_c, v_c = _ctx_kv(cache_mla_ckv[:, ic].reshape(bs * lc, MLA_KV_LORA),
                               cache_mla_kpe[:, ic].reshape(bs * lc, MLA_ROPE), uk_ext, uv, place)
            o = _attention_aug([sq(q_lat, bs, ls), sq(q_ctx, bs, ls)], [sq(k, bs, ls), sq(k_c, bs, lc)],
                               [sq(v, bs, ls), sq(v_c, bs, lc)], MLA_HEADS, MLA_QK_PAD, MLA_SCALE, 512,
                               heads_per_step=8)
            xs = _outproj_residual([o.reshape(bs * ls, -1)], w_out, xs, g1, mod_s, ts, 2)

        w1 = w_mlp_in[layer].astype(BF16)
        w2 = w_mlp_out[layer].astype(BF16)
        xp = _mlp_residual(xp, g2, g3, mod_p, w1, w2, tp)
        xs = _mlp_residual(xs, g2, g3, mod_s, w1, w2, ts)

    return (xp.reshape(bp, lp, D_MODEL), xs.reshape(bs, ls, D_MODEL),
            jnp.stack(ret_f, axis=1), jnp.stack(ret_b, axis=1),
            jnp.stack(na_k, axis=1), jnp.stack(na_v, axis=1),
            jnp.stack(mla_ckv, axis=1), jnp.stack(mla_kpe, axis=1))
```

```python
import functools

import numpy as np
import jax
import jax.numpy as jnp
from jax import lax
from jax.experimental import pallas as pl
from jax.experimental.pallas import tpu as pltpu

D_MODEL = 1024
DEPTH = 4
GRID_W = 64
HEAD_DIM = 64
N_HEADS = D_MODEL // HEAD_DIM
H_RET = N_HEADS // 2
H_NA = N_HEADS - H_RET
RET_CHUNK = 128
WIN_R = 8
WIN_C = 16
MLA_HEADS = N_HEADS
MLA_Q_LORA = 384
MLA_KV_LORA = 256
MLA_NOPE = 64
MLA_ROPE = 32
MLA_V = 64
MLA_SCALE = (MLA_NOPE + MLA_ROPE) ** -0.5
ROPE_BASE = 10000.0
D_FF = 4 * D_MODEL
EPS = 1e-6
NEG_INF = -1e30

LANES = 128
HEAD_PAIR = 2 * HEAD_DIM
MLA_QK_PAD = 128
VMEM_LIMIT = 56 * 1024 * 1024

F32 = jnp.float32
BF16 = jnp.bfloat16


def _cparams(*sem):
    return pltpu.CompilerParams(dimension_semantics=sem, vmem_limit_bytes=VMEM_LIMIT)


def _dot(a, b):
    return jnp.dot(a, b, preferred_element_type=F32)


def _dot_nt(a, b):
    return lax.dot_general(a, b, (((1,), (1,)), ((), ())), preferred_element_type=F32)


def _dot_tn(a, b):
    return lax.dot_general(a, b, (((0,), (0,)), ((), ())), preferred_element_type=F32)


def _rms(x, g):
    ms = jnp.mean(x * x, axis=-1, keepdims=True)
    return x * lax.rsqrt(ms + EPS) * g


def _modulate(x, g, shift, scale):
    return _rms(x, g) * (1.0 + scale) + shift


def _mod_kernel(c_ref, w_ref, b_ref, o_ref):
    c = c_ref[...]
    s = c / (1.0 + jnp.exp(-c))
    o_ref[0] = _dot(s.astype(BF16), w_ref[0].astype(BF16)) + b_ref[0]


def _modulation_all(cond, w_ada, b_ada):
    r = cond.shape[0]
    tn = 1536
    out = pl.pallas_call(
        _mod_kernel,
        grid=(DEPTH, 6 * D_MODEL // tn),
        in_specs=[
            pl.BlockSpec((r, D_MODEL), lambda l, j: (0, 0)),
            pl.BlockSpec((1, D_MODEL, tn), lambda l, j: (l, 0, j)),
            pl.BlockSpec((1, 1, tn), lambda l, j: (l, 0, j)),
        ],
        out_specs=pl.BlockSpec((1, r, tn), lambda l, j: (l, 0, j)),
        out_shape=jax.ShapeDtypeStruct((DEPTH, r, 6 * D_MODEL), F32),
        compiler_params=_cparams("parallel", "parallel"),
        name="modulation",
    )(cond, w_ada, b_ada.reshape(DEPTH, 1, 6 * D_MODEL))
    return out.reshape(DEPTH, r, 6, D_MODEL)


def _mod_spec(tiles_per_row):
    return pl.BlockSpec((1, 6, D_MODEL), lambda i: (i // tiles_per_row, 0, 0))


def _proj_even_kernel(x_ref, g_ref, mod_ref, w_ref, *out_refs, mults):
    m = mod_ref[0]
    h = _modulate(x_ref[...], g_ref[...], m[0:1], m[1:2]).astype(BF16)
    width = out_refs[0].shape[-1]
    for i, (o_ref, mult) in enumerate(zip(out_refs, mults)):
        y = _dot(h, w_ref[:, i * width:(i + 1) * width])
        if mult != 1.0:
            y = y * mult
        o_ref[...] = y.astype(o_ref.dtype)


def _proj_even(x, g, mod, w, tokens_per_row, kv_dtype):
    t = x.shape[0]
    tm = min(512, t)
    width = H_RET * HEAD_DIM
    dtypes = [BF16, BF16, BF16, F32, BF16, kv_dtype, kv_dtype]
    mults = (1.0, HEAD_DIM ** -0.5, 1.0, 1.0, 1.0, 1.0, 1.0)
    return pl.pallas_call(
        functools.partial(_proj_even_kernel, mults=mults),
        grid=(t // tm,),
        in_specs=[
            pl.BlockSpec((tm, D_MODEL), lambda i: (i, 0)),
            pl.BlockSpec((1, D_MODEL), lambda i: (0, 0)),
            _mod_spec(tokens_per_row // tm),
            pl.BlockSpec(w.shape, lambda i: (0, 0)),
        ],
        out_specs=[pl.BlockSpec((tm, width), lambda i: (i, 0)) for _ in dtypes],
        out_shape=[jax.ShapeDtypeStruct((t, width), dt) for dt in dtypes],
        compiler_params=_cparams("parallel"),
        name="proj_even",
    )(x, g, mod, w)


def _retention_kernel(lgf_ref, lgb_ref, q_ref, k_ref, v_ref, g_ref, *rest, n_chunks, has_state, emit_state):
    if has_state:
        sf0_ref, sb0_ref = rest[0], rest[1]
        rest = rest[2:]
    o_ref = rest[0]
    rest = rest[1:]
    if emit_state:
        sfo_ref, sbo_ref = rest[0], rest[1]
        rest = rest[2:]
    acc_ref, kv_ref, st_ref, tab_ref = rest

    c = RET_CHUNK
    d = HEAD_DIM
    pair = pl.program_id(0)
    first_half = lax.broadcasted_iota(jnp.int32, (c, HEAD_PAIR), 1) < d

    @pl.when(pl.program_id(1) == 0)
    def _():
        row = lax.broadcasted_iota(jnp.int32, (c, c), 0).astype(F32)
        col = lax.broadcasted_iota(jnp.int32, (c, c), 1).astype(F32)
        diff = row - col
        pos = lax.broadcasted_iota(jnp.int32, (c, HEAD_PAIR), 0).astype(F32)
        for j in range(2):
            lgf = lgf_ref[2 * pair + j]
            lgb = lgb_ref[2 * pair + j]
            tab_ref[j, 0] = (jnp.where(diff >= 0, jnp.exp(lgf * jnp.maximum(diff, 0.0)), 0.0)
                             + jnp.where(diff <= 0, jnp.exp(lgb * jnp.maximum(-diff, 0.0)), 0.0))
            tab_ref[j, 1] = jnp.where(first_half, jnp.exp(lgf * (c - 1 - pos)), jnp.exp(lgb * pos))
            tab_ref[j, 2] = jnp.where(first_half, jnp.exp(lgf * (pos + 1.0)), jnp.exp(lgb * (c - pos)))

    cdecs = []
    for j in range(2):
        lgf = lgf_ref[2 * pair + j]
        lgb = lgb_ref[2 * pair + j]
        cdecs.append((jnp.exp(jnp.full((1, 1), lgf * c, F32)), jnp.exp(jnp.full((1, 1), lgb * c, F32))))

    def doubled(tile):
        t = tile.astype(F32)
        r = pltpu.roll(t, d, 1)
        return jnp.where(first_half, t, r), jnp.where(first_half, r, t)

    def chunk_rows(n):
        return pl.ds(pl.multiple_of(n * c, c), c)

    halves = (first_half, jnp.logical_not(first_half))
    same_head = ((lax.broadcasted_iota(jnp.int32, (HEAD_PAIR, HEAD_PAIR), 0) < d)
                 == (lax.broadcasted_iota(jnp.int32, (HEAD_PAIR, HEAD_PAIR), 1) < d))
    head_mean = jnp.where(same_head, 1.0 / d, 0.0).astype(BF16)

    def group_mean(x):
        hi = x.astype(BF16)
        lo = (x - hi.astype(F32)).astype(BF16)
        return _dot(hi, head_mean) + _dot(lo, head_mean)

    def intra(n, carry):
        rows = chunk_rows(n)
        q, k, v = q_ref[0, rows, :], k_ref[0, rows, :], v_ref[0, rows, :]
        kf = k.astype(F32)
        k2 = doubled(k)
        o = None
        for j in range(2):
            kj = jnp.where(halves[j], kf, 0.0).astype(BF16)
            s = (_dot_nt(q, kj) * tab_ref[j, 0]).astype(BF16)
            oj = _dot(s, v)
            o = oj if o is None else jnp.where(first_half, o, oj)
            kv = _dot_tn((k2[j] * tab_ref[j, 1]).astype(BF16), v)
            kv_ref[n, j] = jnp.where(halves[j], kv, 0.0)
        acc_ref[rows, :] = o
        return carry

    lax.fori_loop(0, n_chunks, intra, 0, unroll=min(4, n_chunks))

    for j in range(2):
        zeros = jnp.zeros((d, d), F32)
        if has_state:
            place = (lambda s: jnp.concatenate([s, zeros], axis=1)) if j == 0 else (
                lambda s: jnp.concatenate([zeros, s], axis=1))
            sf0, sb0 = place(sf0_ref[0, 0, j]), place(sb0_ref[0, 0, j])
        else:
            sf0 = sb0 = jnp.zeros((d, HEAD_PAIR), F32)

        def scan_f(n, state):
            st_ref[n, 2 * j * d:(2 * j + 1) * d, :] = state.astype(BF16)
            return cdecs[j][0] * state + kv_ref[n, j, 0:d, :]

        def scan_b(i, state):
            n = n_chunks - 1 - i
            st_ref[n, (2 * j + 1) * d:(2 * j + 2) * d, :] = state.astype(BF16)
            return cdecs[j][1] * state + kv_ref[n, j, d:2 * d, :]

        fin_f = lax.fori_loop(0, n_chunks, scan_f, sf0)
        fin_b = lax.fori_loop(0, n_chunks, scan_b, sb0)
        if emit_state:
            sfo_ref[0, j] = fin_f[:, j * d:(j + 1) * d]
            sbo_ref[0, j] = fin_b[:, j * d:(j + 1) * d]

    def finish(n, carry):
        rows = chunk_rows(n)
        q2 = doubled(q_ref[0, rows, :])
        qd = jnp.concatenate([(q2[j] * tab_ref[j, 2]).astype(BF16) for j in range(2)], axis=1)
        o = acc_ref[rows, :] + _dot(qd, st_ref[n])
        cen = o - group_mean(o)
        on = cen * lax.rsqrt(group_mean(cen * cen) + EPS)
        gate = g_ref[0, rows, :]
        o_ref[0, rows, :] = (on * (gate / (1.0 + jnp.exp(-gate)))).astype(o_ref.dtype)
        return carry

    lax.fori_loop(0, n_chunks, finish, 0, unroll=min(8, n_chunks))


def _retention(q, k, v, g, lg_f, lg_b, state_f, state_b, ia, emit_state):
    b, l, _ = q.shape
    has_state = state_f is not None
    blk = lambda: pl.BlockSpec((1, l, HEAD_PAIR), lambda p, bi: (bi, 0, p))
    smem = pl.BlockSpec(memory_space=pltpu.SMEM)
    in_specs = [smem, smem, blk(), blk(), blk(), blk()]
    args = [lg_f, lg_b, q, k, v, g]
    if has_state:
        st = lambda: pl.BlockSpec((1, 1, 2, HEAD_DIM, HEAD_DIM), lambda p, bi: (bi, ia, p, 0, 0))
        in_specs += [st(), st()]
        args += [state_f, state_b]
    out_specs = [blk()]
    out_shape = [jax.ShapeDtypeStruct((b, l, H_RET * HEAD_DIM), BF16)]
    if emit_state:
        so = lambda: pl.BlockSpec((1, 2, HEAD_DIM, HEAD_DIM), lambda p, bi: (bi, p, 0, 0))
        out_specs += [so(), so()]
        out_shape += [jax.ShapeDtypeStruct((b, H_RET, HEAD_DIM, HEAD_DIM), F32)] * 2
    return pl.pallas_call(
        functools.partial(_retention_kernel, n_chunks=l // RET_CHUNK, has_state=has_state,
                          emit_state=emit_state),
        grid=(H_RET // 2, b),
        in_specs=in_specs,
        out_specs=out_specs,
        out_shape=out_shape,
        scratch_shapes=[pltpu.VMEM((l, HEAD_PAIR), F32),
                        pltpu.VMEM((l // RET_CHUNK, 2, HEAD_PAIR, HEAD_PAIR), F32),
                        pltpu.VMEM((l // RET_CHUNK, 2 * HEAD_PAIR, HEAD_PAIR), BF16),
                        pltpu.VMEM((2, 3, RET_CHUNK, HEAD_PAIR), F32)],
        compiler_params=_cparams("arbitrary", "arbitrary"),
        name="retention",
    )(*args)


LOG2_E = 1.4426950408889634


def _attn_aug_kernel(*refs, nseg, dq, scale, heads):
    q_refs = refs[0:nseg]
    k_refs = refs[nseg:2 * nseg]
    v_refs = refs[2 * nseg:3 * nseg]
    o_ref = refs[3 * nseg]
    first_half = lax.broadcasted_iota(jnp.int32, (o_ref.shape[1], HEAD_PAIR), 1) < HEAD_DIM
    for pair in range(heads // 2):
        res = []
        for j in range(2 * pair, 2 * pair + 2):
            ql = slice(j * dq, (j + 1) * dq)
            vl = slice(j * HEAD_PAIR, (j + 1) * HEAD_PAIR)
            scores = [_dot_nt(q_refs[s][0, :, ql], k_refs[s][0, :, ql]) for s in range(nseg)]
            m = functools.reduce(jnp.maximum, [jnp.max(s, axis=-1, keepdims=True) for s in scores])
            acc = None
            for s in range(nseg):
                p = jnp.exp2((scores[s] - m) * (scale * LOG2_E)).astype(BF16)
                pv = _dot(p, v_refs[s][0, :, vl])
                acc = pv if acc is None else acc + pv
            res.append(acc / pltpu.roll(acc, HEAD_DIM, 1))
        out = jnp.where(first_half, res[0], pltpu.roll(res[1], HEAD_DIM, 1))
        o_ref[0, :, pair * HEAD_PAIR:(pair + 1) * HEAD_PAIR] = out.astype(o_ref.dtype)


def _attention_aug(qs, ks, vs, n_heads, dq, scale, tq, heads_per_step=2):
    nseg = len(qs)
    b, lq, _ = qs[0].shape
    hs = heads_per_step
    in_specs = [pl.BlockSpec((1, tq, hs * dq), lambda bi, p, qi: (bi, qi, p)) for _ in qs]
    in_specs += [pl.BlockSpec((1, k.shape[1], hs * dq), lambda bi, p, qi: (bi, 0, p)) for k in ks]
    in_specs += [pl.BlockSpec((1, v.shape[1], hs * HEAD_PAIR), lambda bi, p, qi: (bi, 0, p)) for v in vs]
    return pl.pallas_call(
        functools.partial(_attn_aug_kernel, nseg=nseg, dq=dq, scale=scale, heads=hs),
        grid=(b, n_heads // hs, lq // tq),
        in_specs=in_specs,
        out_specs=pl.BlockSpec((1, tq, hs * HEAD_DIM), lambda bi, p, qi: (bi, qi, p)),
        out_shape=jax.ShapeDtypeStruct((b, lq, n_heads * HEAD_DIM), BF16),
        compiler_params=_cparams("parallel", "parallel", "arbitrary"),
        name="attention_mla",
    )(*qs, *ks, *vs)


def _attn_kernel(*refs, nseg, dq, scale, heads, emit_kv, v_stride):
    q_refs = refs[0:nseg]
    k_refs = refs[nseg:2 * nseg]
    v_refs = refs[2 * nseg:3 * nseg]
    o_ref = refs[3 * nseg]
    for pair in range(heads // 2):
        outs = []
        for j in range(2 * pair, 2 * pair + 2):
            ql = slice(j * dq, (j + 1) * dq)
            vl = slice(j * v_stride, j * v_stride + HEAD_DIM)
            scores = [_dot_nt(q_refs[s][0, :, ql], k_refs[s][0, :, ql].astype(BF16)) for s in range(nseg)]
            m = functools.reduce(jnp.maximum, [jnp.max(s, axis=-1, keepdims=True) for s in scores])
            acc = None
            den = None
            for s in range(nseg):
                p = jnp.exp((scores[s] - m) * scale)
                ps = jnp.sum(p, axis=-1, keepdims=True)
                pv = _dot(p.astype(BF16), v_refs[s][0, :, vl].astype(BF16))
                acc = pv if acc is None else acc + pv
                den = ps if den is None else den + ps
            outs.append(acc / den)
        o_ref[0, :, pair * HEAD_PAIR:(pair + 1) * HEAD_PAIR] = jnp.concatenate(outs, axis=-1).astype(o_ref.dtype)
    if emit_kv:
        ko_ref, vo_ref = refs[3 * nseg + 1], refs[3 * nseg + 2]
        for j in range(heads):
            ko_ref[0, j] = k_refs[0][0, :, j * dq:(j + 1) * dq]
            vo_ref[0, j] = v_refs[0][0, :, j * HEAD_DIM:(j + 1) * HEAD_DIM]


def _attention(qs, ks, vs, n_heads, dq, scale, tq, heads_per_step=2, emit_kv=False, v_stride=HEAD_DIM):
    nseg = len(qs)
    b, lq, _ = qs[0].shape
    hs = heads_per_step
    in_specs = [pl.BlockSpec((1, tq, hs * dq), lambda bi, p, qi: (bi, qi, p)) for _ in qs]
    in_specs += [pl.BlockSpec((1, k.shape[1], hs * dq), lambda bi, p, qi: (bi, 0, p)) for k in ks]
    in_specs += [pl.BlockSpec((1, v.shape[1], hs * v_stride), lambda bi, p, qi: (bi, 0, p)) for v in vs]
    out_specs = [pl.BlockSpec((1, tq, hs * HEAD_DIM), lambda bi, p, qi: (bi, qi, p))]
    out_shape = [jax.ShapeDtypeStruct((b, lq, n_heads * HEAD_DIM), BF16)]
    if emit_kv:
        assert nseg == 1 and tq == lq and dq == HEAD_DIM
        lk = ks[0].shape[1]
        kv = lambda: pl.BlockSpec((1, hs, lk, HEAD_DIM), lambda bi, p, qi: (bi, p, 0, 0))
        out_specs += [kv(), kv()]
        out_shape += [jax.ShapeDtypeStruct((b, n_heads, lk, HEAD_DIM), F32)] * 2
    res = pl.pallas_call(
        functools.partial(_attn_kernel, nseg=nseg, dq=dq, scale=scale, heads=hs, emit_kv=emit_kv,
                          v_stride=v_stride),
        grid=(b, n_heads // hs, lq // tq),
        in_specs=in_specs,
        out_specs=out_specs,
        out_shape=out_shape,
        compiler_params=_cparams("parallel", "parallel", "arbitrary"),
        name="attention",
    )(*qs, *ks, *vs)
    return res if emit_kv else res[0]


NA_QROWS = 4
NA_KROWS = NA_QROWS + WIN_R - 1


def _na_block_start(g, rows):
    return np.clip(NA_QROWS * g - WIN_R // 2, 0, rows - NA_KROWS)


def _na_bias_tables(rpb, rows):
    h = rpb.shape[0]
    w = GRID_W
    nd = 2 * WIN_R - 1
    period = 2 * w - 1
    lo = w - WIN_C
    u = jnp.pad(rpb.astype(F32), ((0, 0), (0, 0), (lo, period - lo - (2 * WIN_C - 1))))
    big = jnp.tile(u, (1, 1, w + 1))[..., :w * 2 * w].reshape(h, nd, w, 2 * w)
    colb = big[:, :, ::-1, :w]
    col = np.arange(w)
    win_start = np.clip(col - WIN_C // 2, 0, w - WIN_C)
    valid = (col[None, :] >= win_start[:, None]) & (col[None, :] < win_start[:, None] + WIN_C)
    colb = jnp.where(valid[None, None], colb, NEG_INF)
    neg = jnp.full((h, w, w), NEG_INF, F32)
    nblk = rows // NA_QROWS
    tables = []
    for g in (0, 1, nblk - 1):
        ks = _na_block_start(g, rows)
        blk_rows = []
        for rr in range(NA_QROWS):
            r = NA_QROWS * g + rr
            rs = np.clip(r - WIN_R // 2, 0, rows - WIN_R)
            pieces = []
            for jj in range(NA_KROWS):
                krow = ks + jj
                pieces.append(colb[:, krow - r + WIN_R - 1] if rs <= krow < rs + WIN_R else neg)
            blk_rows.append(jnp.concatenate(pieces, axis=-1))
        tables.append(jnp.concatenate(blk_rows, axis=1))
    return jnp.stack(tables, axis=1)


def _na_latent_kernel(q_ref, k_ref, v_ref, kc_ref, vc_ref, bias_ref, o_ref, *, rows, heads):
    scale = HEAD_DIM ** -0.5
    nq = NA_QROWS * GRID_W
    nk = NA_KROWS * GRID_W
    nblk = rows // NA_QROWS

    def body(g, carry):
        ks = jnp.clip(NA_QROWS * g - WIN_R // 2, 0, rows - NA_KROWS)
        case = jnp.where(g == 0, 0, jnp.where(g == nblk - 1, 2, 1))
        qrows = pl.ds(pl.multiple_of(g * nq, nq), nq)
        krows = pl.ds(pl.multiple_of(ks * GRID_W, GRID_W), nk)
        outs = []
        for j in range(heads):
            lanes = slice(j * HEAD_DIM, (j + 1) * HEAD_DIM)
            q = q_ref[0, qrows, lanes]
            s_loc = _dot_nt(q, k_ref[0, krows, lanes]) * scale + bias_ref[j, case]
            s_ctx = _dot_nt(q, kc_ref[0, 0, j].astype(BF16)) * scale
            m = jnp.maximum(jnp.max(s_loc, axis=-1, keepdims=True), jnp.max(s_ctx, axis=-1, keepdims=True))
            p_loc = jnp.exp(s_loc - m)
            p_ctx = jnp.exp(s_ctx - m)
            den = jnp.sum(p_loc, axis=-1, keepdims=True) + jnp.sum(p_ctx, axis=-1, keepdims=True)
            acc = (_dot(p_loc.astype(BF16), v_ref[0, krows, lanes])
                   + _dot(p_ctx.astype(BF16), vc_ref[0, 0, j].astype(BF16)))
            outs.append(acc / den)
        o_ref[0, qrows, :] = jnp.concatenate(outs, axis=-1).astype(o_ref.dtype)
        return carry

    lax.fori_loop(0, nblk, body, 0)


def _na_latent(q, k, v, cache_k, cache_v, ia, bias):
    b, l, _ = q.shape
    lc = cache_k.shape[3]
    rows = l // GRID_W
    assert rows % NA_QROWS == 0 and rows >= NA_KROWS + NA_QROWS
    hs = 4
    blk = lambda: pl.BlockSpec((1, l, hs * HEAD_DIM), lambda p, bi: (bi, 0, p))
    ctx = lambda: pl.BlockSpec((1, 1, hs, lc, HEAD_DIM), lambda p, bi: (bi, ia, p, 0, 0))
    return pl.pallas_call(
        functools.partial(_na_latent_kernel, rows=rows, heads=hs),
        grid=(H_NA // hs, b),
        in_specs=[blk(), blk(), blk(), ctx(), ctx(),
                  pl.BlockSpec((hs,) + bias.shape[1:], lambda p, bi: (p, 0, 0, 0))],
        out_specs=blk(),
        out_shape=jax.ShapeDtypeStruct((b, l, H_NA * HEAD_DIM), BF16),
        compiler_params=_cparams("parallel", "parallel"),
        name="na_latent",
    )(q, k, v, cache_k, cache_v, bias)


def _outproj_kernel(*refs, n_in, gate_i):
    a_refs = refs[:n_in]
    w_ref, x_ref, g_ref, mod_ref, o_ref = refs[n_in:]
    y = None
    k0 = 0
    for a_ref in a_refs:
        kw = a_ref.shape[-1]
        part = _dot(a_ref[...], w_ref[k0:k0 + kw, :])
        y = part if y is None else y + part
        k0 += kw
    m = mod_ref[0]
    o_ref[...] = x_ref[...] + m[gate_i:gate_i + 1] * _rms(y, g_ref[...])


def _outproj_residual(acts, w, x, g, mod, tokens_per_row, gate_i):
    t = x.shape[0]
    tm = min(512, t)
    in_specs = [pl.BlockSpec((tm, a.shape[1]), lambda i: (i, 0)) for a in acts]
    in_specs += [
        pl.BlockSpec(w.shape, lambda i: (0, 0)),
        pl.BlockSpec((tm, D_MODEL), lambda i: (i, 0)),
        pl.BlockSpec((1, D_MODEL), lambda i: (0, 0)),
        _mod_spec(tokens_per_row // tm),
    ]
    return pl.pallas_call(
        functools.partial(_outproj_kernel, n_in=len(acts), gate_i=gate_i),
        grid=(t // tm,),
        in_specs=in_specs,
        out_specs=pl.BlockSpec((tm, D_MODEL), lambda i: (i, 0)),
        out_shape=jax.ShapeDtypeStruct((t, D_MODEL), F32),
        compiler_params=_cparams("parallel"),
        name="outproj_residual",
    )(*acts, w, x, g, mod)


def _mlp_kernel(x_ref, gin_ref, gout_ref, mod_ref, w1_ref, w2_ref, o_ref, h_ref, acc_ref):
    f = pl.program_id(1)

    @pl.when(f == 0)
    def _():
        m = mod_ref[0]
        h_ref[...] = _modulate(x_ref[...], gin_ref[...], m[3:4], m[4:5]).astype(BF16)

    a = jnp.maximum(_dot(h_ref[...], w1_ref[...]), 0.0)
    part = _dot((a * a).astype(BF16), w2_ref[...])

    @pl.when(f == 0)
    def _():
        acc_ref[...] = part

    @pl.when(f > 0)
    def _():
        acc_ref[...] += part

    @pl.when(f == pl.num_programs(1) - 1)
    def _():
        m = mod_ref[0]
        o_ref[...] = x_ref[...] + m[5:6] * _rms(acc_ref[...], gout_ref[...])


def _mlp_residual(x, g_in, g_out, mod, w1, w2, tokens_per_row):
    t = x.shape[0]
    tm = min(1024, t)
    tf = 2048
    return pl.pallas_call(
        _mlp_kernel,
        grid=(t // tm, D_FF // tf),
        in_specs=[
            pl.BlockSpec((tm, D_MODEL), lambda i, f: (i, 0)),
            pl.BlockSpec((1, D_MODEL), lambda i, f: (0, 0)),
            pl.BlockSpec((1, D_MODEL), lambda i, f: (0, 0)),
            pl.BlockSpec((1, 6, D_MODEL), lambda i, f: (i // (tokens_per_row // tm), 0, 0)),
            pl.BlockSpec((D_MODEL, tf), lambda i, f: (0, f)),
            pl.BlockSpec((tf, D_MODEL), lambda i, f: (f, 0)),
        ],
        out_specs=pl.BlockSpec((tm, D_MODEL), lambda i, f: (i, 0)),
        out_shape=jax.ShapeDtypeStruct((t, D_MODEL), F32),
        scratch_shapes=[pltpu.VMEM((tm, D_MODEL), BF16), pltpu.VMEM((tm, D_MODEL), F32)],
        compiler_params=_cparams("parallel", "arbitrary"),
        name="mlp_residual",
    )(x, g_in, g_out, mod, w1, w2)


def _value_ones(n):
    lane = lax.broadcasted_iota(jnp.int32, (1, n), 1)
    return jnp.where((lane & HEAD_DIM) != 0, 1.0, 0.0)


def _rope_lanes(y, table, keep):
    lane = lax.broadcasted_iota(jnp.int32, y.shape, 1)
    prod = y * table
    rot = prod + pltpu.roll(prod, LANES - MLA_ROPE, 1)
    return jnp.where((lane >= MLA_NOPE) & (lane < MLA_NOPE + MLA_ROPE), rot, keep)


def _proj_mla_kernel(*refs, rope):
    (x_ref, g_ref, mod_ref, win_ref, qn_ref, kvn_ref, wuq_ref, wuk_ref, wuv_ref) = refs[:9]
    refs = refs[9:]
    if rope:
        tab_ref, refs = refs[0], refs[1:]
        qlat_ref, refs = refs[0], refs[1:]
    qctx_ref, k_ref, v_ref, ckv_ref, kpe_ref = refs

    m = mod_ref[0]
    h = _modulate(x_ref[...], g_ref[...], m[0:1], m[1:2]).astype(BF16)
    y = _dot(h, win_ref[...])
    c_q = y[:, :MLA_Q_LORA]
    c_kv = _rms(y[:, MLA_Q_LORA:MLA_Q_LORA + MLA_KV_LORA], kvn_ref[...])
    pe_blk = y[:, MLA_Q_LORA + MLA_KV_LORA:]
    ckv_ref[...] = c_kv
    kpe_ref[...] = pe_blk[:, MLA_NOPE:MLA_NOPE + MLA_ROPE]

    lane = lax.broadcasted_iota(jnp.int32, pe_blk.shape, 1)
    if rope:
        table = tab_ref[...]
        k_pe = _rope_lanes(pe_blk, table, jnp.zeros_like(pe_blk))
    else:
        k_pe = jnp.where((lane >= MLA_NOPE) & (lane < MLA_NOPE + MLA_ROPE), pe_blk, 0.0)

    q = _dot(_rms(c_q, qn_ref[...]).astype(BF16), wuq_ref[...])
    c_kv_b = c_kv.astype(BF16)
    kn = _dot(c_kv_b, wuk_ref[...])
    v_ref[...] = (_dot(c_kv_b, wuv_ref[...]) + _value_ones(v_ref.shape[-1])).astype(v_ref.dtype)
    for hd in range(MLA_HEADS):
        lanes = slice(hd * MLA_QK_PAD, (hd + 1) * MLA_QK_PAD)
        qh = q[:, lanes]
        qctx_ref[:, lanes] = qh.astype(qctx_ref.dtype)
        if rope:
            qlat_ref[:, lanes] = _rope_lanes(qh, table, qh).astype(qlat_ref.dtype)
        k_ref[:, lanes] = (kn[:, lanes] + k_pe).astype(k_ref.dtype)


def _proj_mla(x, g, mod, w_in, q_norm, kv_norm, w_uq, w_uk, w_uv, tokens_per_row, rope_table):
    t = x.shape[0]
    tm = min(512, t)
    rope = rope_table is not None
    full = lambda a: pl.BlockSpec(a.shape, lambda i: (0,) * a.ndim)
    tok = lambda n: pl.BlockSpec((tm, n), lambda i: (i, 0))
    in_specs = [tok(D_MODEL), full(g), _mod_spec(tokens_per_row // tm), full(w_in), full(q_norm),
                full(kv_norm), full(w_uq), full(w_uk), full(w_uv)]
    args = [x, g, mod, w_in, q_norm, kv_norm, w_uq, w_uk, w_uv]
    qk = MLA_HEADS * MLA_QK_PAD
    out_specs, out_shape = [], []
    if rope:
        tiles_per_seq = rope_table.shape[0] // tm
        in_specs.append(pl.BlockSpec((tm, LANES), lambda i: (i % tiles_per_seq, 0)))
        args.append(rope_table)
        out_specs.append(tok(qk))
        out_shape.append(jax.ShapeDtypeStruct((t, qk), BF16))
    out_specs += [tok(qk), tok(qk), tok(MLA_HEADS * HEAD_PAIR), tok(MLA_KV_LORA), tok(MLA_ROPE)]
    out_shape += [jax.ShapeDtypeStruct((t, qk), BF16), jax.ShapeDtypeStruct((t, qk), BF16),
                  jax.ShapeDtypeStruct((t, MLA_HEADS * HEAD_PAIR), BF16),
                  jax.ShapeDtypeStruct((t, MLA_KV_LORA), F32), jax.ShapeDtypeStruct((t, MLA_ROPE), F32)]
    return pl.pallas_call(
        functools.partial(_proj_mla_kernel, rope=rope),
        grid=(t // tm,),
        in_specs=in_specs,
        out_specs=out_specs,
        out_shape=out_shape,
        compiler_params=_cparams("parallel"),
        name="proj_mla",
    )(*args)


def _ctx_kv_kernel(ckv_ref, kpe_ref, wuk_ref, wuv_ref, place_ref, k_ref, v_ref):
    c = ckv_ref[...].astype(BF16)
    kn = _dot(c, wuk_ref[...])
    k_pe = _dot(kpe_ref[...].astype(BF16), place_ref[...])
    v_ref[...] = (_dot(c, wuv_ref[...]) + _value_ones(v_ref.shape[-1])).astype(v_ref.dtype)
    for hd in range(MLA_HEADS):
        lanes = slice(hd * MLA_QK_PAD, (hd + 1) * MLA_QK_PAD)
        k_ref[:, lanes] = (kn[:, lanes] + k_pe).astype(k_ref.dtype)


def _ctx_kv(ckv, kpe, w_uk, w_uv, place):
    t = ckv.shape[0]
    tm = min(512, t)
    full = lambda a: pl.BlockSpec(a.shape, lambda i: (0,) * a.ndim)
    tok = lambda n: pl.BlockSpec((tm, n), lambda i: (i, 0))
    return pl.pallas_call(
        _ctx_kv_kernel,
        grid=(t // tm,),
        in_specs=[tok(MLA_KV_LORA), tok(MLA_ROPE), full(w_uk), full(w_uv), full(place)],
        out_specs=[tok(MLA_HEADS * MLA_QK_PAD), tok(MLA_HEADS * HEAD_PAIR)],
        out_shape=[jax.ShapeDtypeStruct((t, MLA_HEADS * MLA_QK_PAD), BF16),
                   jax.ShapeDtypeStruct((t, MLA_HEADS * HEAD_PAIR), BF16)],
        compiler_params=_cparams("parallel"),
        name="mla_ctx_kv",
    )(ckv, kpe, w_uk, w_uv, place)


def _pair_swap(w):
    return w.reshape(w.shape[:-1] + (w.shape[-1] // 2, 2))[..., ::-1].reshape(w.shape)


def _mla_weights(w_in, w_uq, w_uk, w_uv):
    k_pe_cols = w_in[:, MLA_Q_LORA + MLA_KV_LORA:]
    w_in_ext = jnp.concatenate(
        [w_in[:, :MLA_Q_LORA + MLA_KV_LORA], jnp.zeros((D_MODEL, MLA_NOPE), w_in.dtype),
         k_pe_cols, _pair_swap(k_pe_cols)], axis=1).astype(BF16)
    uq = w_uq.reshape(MLA_Q_LORA, MLA_HEADS, MLA_NOPE + MLA_ROPE)
    uq_ext = jnp.concatenate([uq, _pair_swap(uq[..., MLA_NOPE:])], axis=-1)
    uq_ext = uq_ext.reshape(MLA_Q_LORA, MLA_HEADS * MLA_QK_PAD).astype(BF16)
    uk = w_uk.reshape(MLA_KV_LORA, MLA_HEADS, MLA_NOPE)
    uk_ext = jnp.concatenate([uk, jnp.zeros((MLA_KV_LORA, MLA_HEADS, MLA_QK_PAD - MLA_NOPE), uk.dtype)], axis=-1)
    uk_ext = uk_ext.reshape(MLA_KV_LORA, MLA_HEADS * MLA_QK_PAD).astype(BF16)
    uv = w_uv.reshape(MLA_KV_LORA, MLA_HEADS, MLA_V)
    uv_ext = jnp.concatenate([uv, jnp.zeros((MLA_KV_LORA, MLA_HEADS, HEAD_PAIR - MLA_V), uv.dtype)], axis=-1)
    uv_ext = uv_ext.reshape(MLA_KV_LORA, MLA_HEADS * HEAD_PAIR).astype(BF16)
    return w_in_ext, uq_ext, uk_ext, uv_ext


def _rope_table(l):
    t = jnp.arange(l)
    row = (t // GRID_W).astype(F32)
    col = (t % GRID_W).astype(F32)
    nf = MLA_ROPE // 4
    inv = ROPE_BASE ** (-jnp.arange(nf, dtype=F32) / nf)
    ang = jnp.concatenate([row[:, None] * inv, col[:, None] * inv], axis=-1)
    cos, sin = jnp.cos(ang), jnp.sin(ang)
    cc = jnp.repeat(cos, 2, axis=-1)
    ss = jnp.stack([-sin, sin], axis=-1).reshape(l, MLA_ROPE)
    return jnp.concatenate([jnp.ones((l, MLA_NOPE), F32), cc, ss], axis=-1)


def kernel(x_prompt, x_sample, state_ret_fwd, state_ret_bwd, cache_na_k, cache_na_v, cache_mla_ckv,
           cache_mla_kpe, c, c_ctx, w_ada, b_ada, norm_gains, w_mlp_in, w_mlp_out, w_in_ac, w_out_ac,
           ret_decay_fwd, ret_decay_bwd, na_rpb, w_in_c, mla_q_norm, mla_kv_norm, w_uq, w_uk, w_uv, w_out_c):
    bp, lp, _ = x_prompt.shape
    bs, ls, _ = x_sample.shape
    lc = cache_mla_ckv.shape[2]
    xp = x_prompt.reshape(bp * lp, D_MODEL)
    xs = x_sample.reshape(bs * ls, D_MODEL)

    n_cond = bs + 1
    cond = jnp.concatenate([c, c_ctx[None]], axis=0)
    mods = _modulation_all(cond, w_ada, b_ada)
    rope_table = _rope_table(ls)
    place = jnp.zeros((MLA_ROPE, MLA_QK_PAD), BF16).at[
        jnp.arange(MLA_ROPE), MLA_NOPE + jnp.arange(MLA_ROPE)].set(1.0)

    ret_f, ret_b, na_k, na_v, mla_ckv, mla_kpe = [], [], [], [], [], []
    for layer in range(DEPTH):
        mod_s = mods[layer, :bs]
        mod_p = mods[layer, bs:n_cond]
        g = norm_gains[layer]
        g0, g1, g2, g3 = (g[i:i + 1] for i in range(4))
        tp, ts = bp * lp, ls
        if layer % 2 == 0:
            ia = layer // 2
            w_in = w_in_ac[ia].astype(BF16)
            w_out = w_out_ac[ia].astype(BF16)
            lg_f = jnp.log1p(-jnp.exp2(ret_decay_fwd[ia].astype(F32)))
            lg_b = jnp.log1p(-jnp.exp2(ret_decay_bwd[ia].astype(F32)))
            seq = lambda a, b, l: a.reshape(b, l, a.shape[-1])

            rq, rk, rv, rg, nq, nk, nv = [seq(a, bp, lp) for a in _proj_even(xp, g0, mod_p, w_in, tp, F32)]
            o_ret, s_f, s_b = _retention(rq, rk, rv, rg, lg_f, lg_b, None, None, ia, True)
            o_na, k_out, v_out = _attention([nq], [nk], [nv], H_NA, HEAD_DIM, HEAD_DIM ** -0.5, lp,
                                             heads_per_step=H_NA, emit_kv=True)
            ret_f.append(s_f)
            ret_b.append(s_b)
            na_k.append(k_out)
            na_v.append(v_out)
            xp = _outproj_residual([o_ret.reshape(tp, -1), o_na.reshape(tp, -1)], w_out, xp, g1, mod_p, tp, 2)

            rq, rk, rv, rg, nq, nk, nv = [seq(a, bs, ls) for a in _proj_even(xs, g0, mod_s, w_in, ts, BF16)]
            o_ret = _retention(rq, rk, rv, rg, lg_f, lg_b, state_ret_fwd, state_ret_bwd, ia, False)[0]
            o_na = _na_latent(nq, nk, nv, cache_na_k, cache_na_v, ia, _na_bias_tables(na_rpb[ia], ls // GRID_W))
            xs = _outproj_residual([o_ret.reshape(bs * ls, -1), o_na.reshape(bs * ls, -1)], w_out, xs, g1,
                                   mod_s, ts, 2)
        else:
            ic = layer // 2
            w_in_ext, uq_ext, uk_ext, uv = _mla_weights(w_in_c[ic], w_uq[ic], w_uk[ic], w_uv[ic])
            w_out = w_out_c[ic].astype(BF16)
            qn, kvn = mla_q_norm[ic][None], mla_kv_norm[ic][None]

            q, k, v, ckv, kpe = _proj_mla(xp, g0, mod_p, w_in_ext, qn, kvn, uq_ext, uk_ext, uv, tp, None)
            mla_ckv.append(ckv.reshape(bp, lp, MLA_KV_LORA))
            mla_kpe.append(kpe.reshape(bp, lp, MLA_ROPE))
            sq = lambda a, b, l: a.reshape(b, l, a.shape[-1])
            o = _attention([sq(q, bp, lp)], [sq(k, bp, lp)], [sq(v, bp, lp)], MLA_HEADS, MLA_QK_PAD,
                           MLA_SCALE, lp, heads_per_step=8, v_stride=HEAD_PAIR)
            xp = _outproj_residual([o.reshape(tp, -1)], w_out, xp, g1, mod_p, tp, 2)

            q_lat, q_ctx, k, v, _, _ = _proj_mla(xs, g0, mod_s, w_in_ext, qn, kvn, uq_ext, uk_ext, uv, ts,
                                                 rope_table)
            k_c, v_c = _ctx_kv(cache_mla_ckv[:, ic].reshape(bs * lc, MLA_KV_LORA),
                               cache_mla_kpe[:, ic].reshape(bs * lc, MLA_ROPE), uk_ext, uv, place)
            o = _attention_aug([sq(q_lat, bs, ls), sq(q_ctx, bs, ls)], [sq(k, bs, ls), sq(k_c, bs, lc)],
                               [sq(v, bs, ls), sq(v_c, bs, lc)], MLA_HEADS, MLA_QK_PAD, MLA_SCALE, 1024,
                               heads_per_step=8)
            xs = _outproj_residual([o.reshape(bs * ls, -1)], w_out, xs, g1, mod_s, ts, 2)

        w1 = w_mlp_in[layer].astype(BF16)
        w2 = w_mlp_out[layer].astype(BF16)
        xp = _mlp_residual(xp, g2, g3, mod_p, w1, w2, tp)
        xs = _mlp_residual(xs, g2, g3, mod_s, w1, w2, ts)

    return (xp.reshape(bp, lp, D_MODEL), xs.reshape(bs, ls, D_MODEL),
            jnp.stack(ret_f, axis=1), jnp.stack(ret_b, axis=1),
            jnp.stack(na_k, axis=1), jnp.stack(na_v, axis=1),
            jnp.stack(mla_ckv, axis=1), jnp.stack(mla_kpe, axis=1))
```

```python
import functools

import numpy as np
import jax
import jax.numpy as jnp
from jax import lax
from jax.experimental import pallas as pl
from jax.experimental.pallas import tpu as pltpu

D_MODEL = 1024
DEPTH = 4
GRID_W = 64
HEAD_DIM = 64
N_HEADS = D_MODEL // HEAD_DIM
H_RET = N_HEADS // 2
H_NA = N_HEADS - H_RET
RET_CHUNK = 128
WIN_R = 8
WIN_C = 16
MLA_HEADS = N_HEADS
MLA_Q_LORA = 384
MLA_KV_LORA = 256
MLA_NOPE = 64
MLA_ROPE = 32
MLA_V = 64
MLA_SCALE = (MLA_NOPE + MLA_ROPE) ** -0.5
ROPE_BASE = 10000.0
D_FF = 4 * D_MODEL
EPS = 1e-6
NEG_INF = -1e30

LANES = 128
HEAD_PAIR = 2 * HEAD_DIM
MLA_QK_PAD = 128
VMEM_LIMIT = 56 * 1024 * 1024

F32 = jnp.float32
BF16 = jnp.bfloat16


def _cparams(*sem):
    return pltpu.CompilerParams(dimension_semantics=sem, vmem_limit_bytes=VMEM_LIMIT)


def _dot(a, b):
    return jnp.dot(a, b, preferred_element_type=F32)


def _dot_nt(a, b):
    return lax.dot_general(a, b, (((1,), (1,)), ((), ())), preferred_element_type=F32)


def _dot_tn(a, b):
    return lax.dot_general(a, b, (((0,), (0,)), ((), ())), preferred_element_type=F32)


def _rms(x, g):
    ms = jnp.mean(x * x, axis=-1, keepdims=True)
    return x * lax.rsqrt(ms + EPS) * g


def _modulate(x, g, shift, scale):
    return _rms(x, g) * (1.0 + scale) + shift


def _mod_kernel(c_ref, w_ref, b_ref, o_ref):
    c = c_ref[...]
    s = c / (1.0 + jnp.exp(-c))
    o_ref[0] = _dot(s.astype(BF16), w_ref[0].astype(BF16)) + b_ref[0]


def _modulation_all(cond, w_ada, b_ada):
    r = cond.shape[0]
    tn = 1536
    out = pl.pallas_call(
        _mod_kernel,
        grid=(DEPTH, 6 * D_MODEL // tn),
        in_specs=[
            pl.BlockSpec((r, D_MODEL), lambda l, j: (0, 0)),
            pl.BlockSpec((1, D_MODEL, tn), lambda l, j: (l, 0, j)),
            pl.BlockSpec((1, 1, tn), lambda l, j: (l, 0, j)),
        ],
        out_specs=pl.BlockSpec((1, r, tn), lambda l, j: (l, 0, j)),
        out_shape=jax.ShapeDtypeStruct((DEPTH, r, 6 * D_MODEL), F32),
        compiler_params=_cparams("parallel", "parallel"),
        name="modulation",
    )(cond, w_ada, b_ada.reshape(DEPTH, 1, 6 * D_MODEL))
    return out.reshape(DEPTH, r, 6, D_MODEL)


def _mod_spec(tiles_per_row):
    return pl.BlockSpec((1, 6, D_MODEL), lambda i: (i // tiles_per_row, 0, 0))


def _proj_even_kernel(x_ref, g_ref, mod_ref, w_ref, *out_refs, mults):
    m = mod_ref[0]
    h = _modulate(x_ref[...], g_ref[...], m[0:1], m[1:2]).astype(BF16)
    width = out_refs[0].shape[-1]
    for i, (o_ref, mult) in enumerate(zip(out_refs, mults)):
        y = _dot(h, w_ref[:, i * width:(i + 1) * width])
        if mult != 1.0:
            y = y * mult
        o_ref[...] = y.astype(o_ref.dtype)


def _proj_even(x, g, mod, w, tokens_per_row, kv_dtype):
    t = x.shape[0]
    tm = min(512, t)
    width = H_RET * HEAD_DIM
    dtypes = [BF16, BF16, BF16, F32, BF16, kv_dtype, kv_dtype]
    mults = (1.0, HEAD_DIM ** -0.5, 1.0, 1.0, 1.0, 1.0, 1.0)
    return pl.pallas_call(
        functools.partial(_proj_even_kernel, mults=mults),
        grid=(t // tm,),
        in_specs=[
            pl.BlockSpec((tm, D_MODEL), lambda i: (i, 0)),
            pl.BlockSpec((1, D_MODEL), lambda i: (0, 0)),
            _mod_spec(tokens_per_row // tm),
            pl.BlockSpec(w.shape, lambda i: (0, 0)),
        ],
        out_specs=[pl.BlockSpec((tm, width), lambda i: (i, 0)) for _ in dtypes],
        out_shape=[jax.ShapeDtypeStruct((t, width), dt) for dt in dtypes],
        compiler_params=_cparams("parallel"),
        name="proj_even",
    )(x, g, mod, w)


def _retention_kernel(lgf_ref, lgb_ref, q_ref, k_ref, v_ref, g_ref, *rest, n_chunks, has_state, emit_state):
    if has_state:
        sf0_ref, sb0_ref = rest[0], rest[1]
        rest = rest[2:]
    o_ref = rest[0]
    rest = rest[1:]
    if emit_state:
        sfo_ref, sbo_ref = rest[0], rest[1]
        rest = rest[2:]
    acc_ref, kv_ref, st_ref, tab_ref = rest

    c = RET_CHUNK
    d = HEAD_DIM
    pair = pl.program_id(0)
    first_half = lax.broadcasted_iota(jnp.int32, (c, HEAD_PAIR), 1) < d

    @pl.when(pl.program_id(1) == 0)
    def _():
        row = lax.broadcasted_iota(jnp.int32, (c, c), 0).astype(F32)
        col = lax.broadcasted_iota(jnp.int32, (c, c), 1).astype(F32)
        diff = row - col
        pos = lax.broadcasted_iota(jnp.int32, (c, HEAD_PAIR), 0).astype(F32)
        for j in range(2):
            lgf = lgf_ref[2 * pair + j]
            lgb = lgb_ref[2 * pair + j]
            tab_ref[j, 0] = (jnp.where(diff >= 0, jnp.exp(lgf * jnp.maximum(diff, 0.0)), 0.0)
                             + jnp.where(diff <= 0, jnp.exp(lgb * jnp.maximum(-diff, 0.0)), 0.0))
            tab_ref[j, 1] = jnp.where(first_half, jnp.exp(lgf * (c - 1 - pos)), jnp.exp(lgb * pos))
            tab_ref[j, 2] = jnp.where(first_half, jnp.exp(lgf * (pos + 1.0)), jnp.exp(lgb * (c - pos)))

    cdecs = []
    for j in range(2):
        lgf = lgf_ref[2 * pair + j]
        lgb = lgb_ref[2 * pair + j]
        cdecs.append((jnp.exp(jnp.full((1, 1), lgf * c, F32)), jnp.exp(jnp.full((1, 1), lgb * c, F32))))

    def doubled(tile):
        t = tile.astype(F32)
        r = pltpu.roll(t, d, 1)
        return jnp.where(first_half, t, r), jnp.where(first_half, r, t)

    def chunk_rows(n):
        return pl.ds(pl.multiple_of(n * c, c), c)

    halves = (first_half, jnp.logical_not(first_half))
    same_head = ((lax.broadcasted_iota(jnp.int32, (HEAD_PAIR, HEAD_PAIR), 0) < d)
                 == (lax.broadcasted_iota(jnp.int32, (HEAD_PAIR, HEAD_PAIR), 1) < d))
    head_mean = jnp.where(same_head, 1.0 / d, 0.0).astype(BF16)

    def group_mean(x):
        hi = x.astype(BF16)
        lo = (x - hi.astype(F32)).astype(BF16)
        return _dot(hi, head_mean) + _dot(lo, head_mean)

    def intra(n, carry):
        rows = chunk_rows(n)
        q, k, v = q_ref[0, rows, :], k_ref[0, rows, :], v_ref[0, rows, :]
        kf = k.astype(F32)
        k2 = doubled(k)
        o = None
        for j in range(2):
            kj = jnp.where(halves[j], kf, 0.0).astype(BF16)
            s = (_dot_nt(q, kj) * tab_ref[j, 0]).astype(BF16)
            oj = _dot(s, v)
            o = oj if o is None else jnp.where(first_half, o, oj)
            kv = _dot_tn((k2[j] * tab_ref[j, 1]).astype(BF16), v)
            kv_ref[n, j] = jnp.where(halves[j], kv, 0.0)
        acc_ref[rows, :] = o
        return carry

    lax.fori_loop(0, n_chunks, intra, 0, unroll=min(4, n_chunks))

    for j in range(2):
        zeros = jnp.zeros((d, d), F32)
        if has_state:
            place = (lambda s: jnp.concatenate([s, zeros], axis=1)) if j == 0 else (
                lambda s: jnp.concatenate([zeros, s], axis=1))
            sf0, sb0 = place(sf0_ref[0, 0, j]), place(sb0_ref[0, 0, j])
        else:
            sf0 = sb0 = jnp.zeros((d, HEAD_PAIR), F32)

        def scan_f(n, state):
            st_ref[n, 2 * j * d:(2 * j + 1) * d, :] = state.astype(BF16)
            return cdecs[j][0] * state + kv_ref[n, j, 0:d, :]

        def scan_b(i, state):
            n = n_chunks - 1 - i
            st_ref[n, (2 * j + 1) * d:(2 * j + 2) * d, :] = state.astype(BF16)
            return cdecs[j][1] * state + kv_ref[n, j, d:2 * d, :]

        fin_f = lax.fori_loop(0, n_chunks, scan_f, sf0)
        fin_b = lax.fori_loop(0, n_chunks, scan_b, sb0)
        if emit_state:
            sfo_ref[0, j] = fin_f[:, j * d:(j + 1) * d]
            sbo_ref[0, j] = fin_b[:, j * d:(j + 1) * d]

    def finish(n, carry):
        rows = chunk_rows(n)
        q2 = doubled(q_ref[0, rows, :])
        qd = jnp.concatenate([(q2[j] * tab_ref[j, 2]).astype(BF16) for j in range(2)], axis=1)
        o = acc_ref[rows, :] + _dot(qd, st_ref[n])
        cen = o - group_mean(o)
        on = cen * lax.rsqrt(group_mean(cen * cen) + EPS)
        gate = g_ref[0, rows, :]
        o_ref[0, rows, :] = (on * (gate / (1.0 + jnp.exp(-gate)))).astype(o_ref.dtype)
        return carry

    lax.fori_loop(0, n_chunks, finish, 0, unroll=min(8, n_chunks))


def _retention(q, k, v, g, lg_f, lg_b, state_f, state_b, ia, emit_state):
    b, l, _ = q.shape
    has_state = state_f is not None
    blk = lambda: pl.BlockSpec((1, l, HEAD_PAIR), lambda p, bi: (bi, 0, p))
    smem = pl.BlockSpec(memory_space=pltpu.SMEM)
    in_specs = [smem, smem, blk(), blk(), blk(), blk()]
    args = [lg_f, lg_b, q, k, v, g]
    if has_state:
        st = lambda: pl.BlockSpec((1, 1, 2, HEAD_DIM, HEAD_DIM), lambda p, bi: (bi, ia, p, 0, 0))
        in_specs += [st(), st()]
        args += [state_f, state_b]
    out_specs = [blk()]
    out_shape = [jax.ShapeDtypeStruct((b, l, H_RET * HEAD_DIM), BF16)]
    if emit_state:
        so = lambda: pl.BlockSpec((1, 2, HEAD_DIM, HEAD_DIM), lambda p, bi: (bi, p, 0, 0))
        out_specs += [so(), so()]
        out_shape += [jax.ShapeDtypeStruct((b, H_RET, HEAD_DIM, HEAD_DIM), F32)] * 2
    return pl.pallas_call(
        functools.partial(_retention_kernel, n_chunks=l // RET_CHUNK, has_state=has_state,
                          emit_state=emit_state),
        grid=(H_RET // 2, b),
        in_specs=in_specs,
        out_specs=out_specs,
        out_shape=out_shape,
        scratch_shapes=[pltpu.VMEM((l, HEAD_PAIR), F32),
                        pltpu.VMEM((l // RET_CHUNK, 2, HEAD_PAIR, HEAD_PAIR), F32),
                        pltpu.VMEM((l // RET_CHUNK, 2 * HEAD_PAIR, HEAD_PAIR), BF16),
                        pltpu.VMEM((2, 3, RET_CHUNK, HEAD_PAIR), F32)],
        compiler_params=_cparams("arbitrary", "arbitrary"),
        name="retention",
    )(*args)


LOG2_E = 1.4426950408889634


def _attn_aug_kernel(*refs, nseg, dq, scale, heads):
    q_refs = refs[0:nseg]
    k_refs = refs[nseg:2 * nseg]
    v_refs = refs[2 * nseg:3 * nseg]
    o_ref = refs[3 * nseg]
    first_half = lax.broadcasted_iota(jnp.int32, (o_ref.shape[1], HEAD_PAIR), 1) < HEAD_DIM
    for pair in range(heads // 2):
        res = []
        for j in range(2 * pair, 2 * pair + 2):
            ql = slice(j * dq, (j + 1) * dq)
            vl = slice(j * HEAD_PAIR, (j + 1) * HEAD_PAIR)
            scores = [_dot_nt(q_refs[s][0, :, ql], k_refs[s][0, :, ql]) for s in range(nseg)]
            m = functools.reduce(jnp.maximum, [jnp.max(s, axis=-1, keepdims=True) for s in scores])
            acc = None
            for s in range(nseg):
                p = jnp.exp2((scores[s] - m) * (scale * LOG2_E)).astype(BF16)
                pv = _dot(p, v_refs[s][0, :, vl])
                acc = pv if acc is None else acc + pv
            res.append(acc / pltpu.roll(acc, HEAD_DIM, 1))
        out = jnp.where(first_half, res[0], pltpu.roll(res[1], HEAD_DIM, 1))
        o_ref[0, :, pair * HEAD_PAIR:(pair + 1) * HEAD_PAIR] = out.astype(o_ref.dtype)


def _attention_aug(qs, ks, vs, n_heads, dq, scale, tq, heads_per_step=2):
    nseg = len(qs)
    b, lq, _ = qs[0].shape
    hs = heads_per_step
    in_specs = [pl.BlockSpec((1, tq, hs * dq), lambda bi, p, qi: (bi, qi, p)) for _ in qs]
    in_specs += [pl.BlockSpec((1, k.shape[1], hs * dq), lambda bi, p, qi: (bi, 0, p)) for k in ks]
    in_specs += [pl.BlockSpec((1, v.shape[1], hs * HEAD_PAIR), lambda bi, p, qi: (bi, 0, p)) for v in vs]
    return pl.pallas_call(
        functools.partial(_attn_aug_kernel, nseg=nseg, dq=dq, scale=scale, heads=hs),
        grid=(b, n_heads // hs, lq // tq),
        in_specs=in_specs,
        out_specs=pl.BlockSpec((1, tq, hs * HEAD_DIM), lambda bi, p, qi: (bi, qi, p)),
        out_shape=jax.ShapeDtypeStruct((b, lq, n_heads * HEAD_DIM), BF16),
        compiler_params=_cparams("parallel", "parallel", "arbitrary"),
        name="attention_mla",
    )(*qs, *ks, *vs)


def _attn_kernel(*refs, nseg, dq, scale, heads, emit_kv, v_stride):
    q_refs = refs[0:nseg]
    k_refs = refs[nseg:2 * nseg]
    v_refs = refs[2 * nseg:3 * nseg]
    o_ref = refs[3 * nseg]
    for pair in range(heads // 2):
        outs = []
        for j in range(2 * pair, 2 * pair + 2):
            ql = slice(j * dq, (j + 1) * dq)
            vl = slice(j * v_stride, j * v_stride + HEAD_DIM)
            scores = [_dot_nt(q_refs[s][0, :, ql], k_refs[s][0, :, ql].astype(BF16)) for s in range(nseg)]
            m = functools.reduce(jnp.maximum, [jnp.max(s, axis=-1, keepdims=True) for s in scores])
            acc = None
            den = None
            for s in range(nseg):
                p = jnp.exp((scores[s] - m) * scale)
                ps = jnp.sum(p, axis=-1, keepdims=True)
                pv = _dot(p.astype(BF16), v_refs[s][0, :, vl].astype(BF16))
                acc = pv if acc is None else acc + pv
                den = ps if den is None else den + ps
            outs.append(acc / den)
        o_ref[0, :, pair * HEAD_PAIR:(pair + 1) * HEAD_PAIR] = jnp.concatenate(outs, axis=-1).astype(o_ref.dtype)
    if emit_kv:
        ko_ref, vo_ref = refs[3 * nseg + 1], refs[3 * nseg + 2]
        for j in range(heads):
            ko_ref[0, j] = k_refs[0][0, :, j * dq:(j + 1) * dq]
            vo_ref[0, j] = v_refs[0][0, :, j * HEAD_DIM:(j + 1) * HEAD_DIM]


def _attention(qs, ks, vs, n_heads, dq, scale, tq, heads_per_step=2, emit_kv=False, v_stride=HEAD_DIM):
    nseg = len(qs)
    b, lq, _ = qs[0].shape
    hs = heads_per_step
    in_specs = [pl.BlockSpec((1, tq, hs * dq), lambda bi, p, qi: (bi, qi, p)) for _ in qs]
    in_specs += [pl.BlockSpec((1, k.shape[1], hs * dq), lambda bi, p, qi: (bi, 0, p)) for k in ks]
    in_specs += [pl.BlockSpec((1, v.shape[1], hs * v_stride), lambda bi, p, qi: (bi, 0, p)) for v in vs]
    out_specs = [pl.BlockSpec((1, tq, hs * HEAD_DIM), lambda bi, p, qi: (bi, qi, p))]
    out_shape = [jax.ShapeDtypeStruct((b, lq, n_heads * HEAD_DIM), BF16)]
    if emit_kv:
        assert nseg == 1 and tq == lq and dq == HEAD_DIM
        lk = ks[0].shape[1]
        kv = lambda: pl.BlockSpec((1, hs, lk, HEAD_DIM), lambda bi, p, qi: (bi, p, 0, 0))
        out_specs += [kv(), kv()]
        out_shape += [jax.ShapeDtypeStruct((b, n_heads, lk, HEAD_DIM), F32)] * 2
    res = pl.pallas_call(
        functools.partial(_attn_kernel, nseg=nseg, dq=dq, scale=scale, heads=hs, emit_kv=emit_kv,
                          v_stride=v_stride),
        grid=(b, n_heads // hs, lq // tq),
        in_specs=in_specs,
        out_specs=out_specs,
        out_shape=out_shape,
        compiler_params=_cparams("parallel", "parallel", "arbitrary"),
        name="attention",
    )(*qs, *ks, *vs)
    return res if emit_kv else res[0]


NA_QROWS = 4
NA_KROWS = NA_QROWS + WIN_R - 1


def _na_block_start(g, rows):
    return np.clip(NA_QROWS * g - WIN_R // 2, 0, rows - NA_KROWS)


def _na_bias_tables(rpb, rows):
    h = rpb.shape[0]
    w = GRID_W
    nd = 2 * WIN_R - 1
    period = 2 * w - 1
    lo = w - WIN_C
    u = jnp.pad(rpb.astype(F32), ((0, 0), (0, 0), (lo, period - lo - (2 * WIN_C - 1))))
    big = jnp.tile(u, (1, 1, w + 1))[..., :w * 2 * w].reshape(h, nd, w, 2 * w)
    colb = big[:, :, ::-1, :w]
    col = np.arange(w)
    win_start = np.clip(col - WIN_C // 2, 0, w - WIN_C)
    valid = (col[None, :] >= win_start[:, None]) & (col[None, :] < win_start[:, None] + WIN_C)
    colb = jnp.where(valid[None, None], colb, NEG_INF)
    neg = jnp.full((h, w, w), NEG_INF, F32)
    nblk = rows // NA_QROWS
    tables = []
    for g in (0, 1, nblk - 1):
        ks = _na_block_start(g, rows)
        blk_rows = []
        for rr in range(NA_QROWS):
            r = NA_QROWS * g + rr
            rs = np.clip(r - WIN_R // 2, 0, rows - WIN_R)
            pieces = []
            for jj in range(NA_KROWS):
                krow = ks + jj
                pieces.append(colb[:, krow - r + WIN_R - 1] if rs <= krow < rs + WIN_R else neg)
            blk_rows.append(jnp.concatenate(pieces, axis=-1))
        tables.append(jnp.concatenate(blk_rows, axis=1))
    return jnp.stack(tables, axis=1)


def _na_latent_kernel(q_ref, k_ref, v_ref, kc_ref, vc_ref, bias_ref, o_ref, *, rows, heads):
    scale = HEAD_DIM ** -0.5
    nq = NA_QROWS * GRID_W
    nk = NA_KROWS * GRID_W
    nblk = rows // NA_QROWS

    def body(g, carry):
        ks = jnp.clip(NA_QROWS * g - WIN_R // 2, 0, rows - NA_KROWS)
        case = jnp.where(g == 0, 0, jnp.where(g == nblk - 1, 2, 1))
        qrows = pl.ds(pl.multiple_of(g * nq, nq), nq)
        krows = pl.ds(pl.multiple_of(ks * GRID_W, GRID_W), nk)
        outs = []
        for j in range(heads):
            lanes = slice(j * HEAD_DIM, (j + 1) * HEAD_DIM)
            q = q_ref[0, qrows, lanes]
            s_loc = _dot_nt(q, k_ref[0, krows, lanes]) * scale + bias_ref[j, case]
            s_ctx = _dot_nt(q, kc_ref[0, 0, j].astype(BF16)) * scale
            m = jnp.maximum(jnp.max(s_loc, axis=-1, keepdims=True), jnp.max(s_ctx, axis=-1, keepdims=True))
            p_loc = jnp.exp(s_loc - m)
            p_ctx = jnp.exp(s_ctx - m)
            den = jnp.sum(p_loc, axis=-1, keepdims=True) + jnp.sum(p_ctx, axis=-1, keepdims=True)
            acc = (_dot(p_loc.astype(BF16), v_ref[0, krows, lanes])
                   + _dot(p_ctx.astype(BF16), vc_ref[0, 0, j].astype(BF16)))
            outs.append(acc / den)
        o_ref[0, qrows, :] = jnp.concatenate(outs, axis=-1).astype(o_ref.dtype)
        return carry

    lax.fori_loop(0, nblk, body, 0)


def _na_latent(q, k, v, cache_k, cache_v, ia, bias):
    b, l, _ = q.shape
    lc = cache_k.shape[3]
    rows = l // GRID_W
    assert rows % NA_QROWS == 0 and rows >= NA_KROWS + NA_QROWS
    hs = 4
    blk = lambda: pl.BlockSpec((1, l, hs * HEAD_DIM), lambda p, bi: (bi, 0, p))
    ctx = lambda: pl.BlockSpec((1, 1, hs, lc, HEAD_DIM), lambda p, bi: (bi, ia, p, 0, 0))
    return pl.pallas_call(
        functools.partial(_na_latent_kernel, rows=rows, heads=hs),
        grid=(H_NA // hs, b),
        in_specs=[blk(), blk(), blk(), ctx(), ctx(),
                  pl.BlockSpec((hs,) + bias.shape[1:], lambda p, bi: (p, 0, 0, 0))],
        out_specs=blk(),
        out_shape=jax.ShapeDtypeStruct((b, l, H_NA * HEAD_DIM), BF16),
        compiler_params=_cparams("parallel", "parallel"),
        name="na_latent",
    )(q, k, v, cache_k, cache_v, bias)


def _outproj_kernel(*refs, n_in, gate_i):
    a_refs = refs[:n_in]
    w_ref, x_ref, g_ref, mod_ref, o_ref = refs[n_in:]
    y = None
    k0 = 0
    for a_ref in a_refs:
        kw = a_ref.shape[-1]
        part = _dot(a_ref[...], w_ref[k0:k0 + kw, :])
        y = part if y is None else y + part
        k0 += kw
    m = mod_ref[0]
    o_ref[...] = x_ref[...] + m[gate_i:gate_i + 1] * _rms(y, g_ref[...])


def _outproj_residual(acts, w, x, g, mod, tokens_per_row, gate_i):
    t = x.shape[0]
    tm = min(1024, t)
    in_specs = [pl.BlockSpec((tm, a.shape[1]), lambda i: (i, 0)) for a in acts]
    in_specs += [
        pl.BlockSpec(w.shape, lambda i: (0, 0)),
        pl.BlockSpec((tm, D_MODEL), lambda i: (i, 0)),
        pl.BlockSpec((1, D_MODEL), lambda i: (0, 0)),
        _mod_spec(tokens_per_row // tm),
    ]
    return pl.pallas_call(
        functools.partial(_outproj_kernel, n_in=len(acts), gate_i=gate_i),
        grid=(t // tm,),
        in_specs=in_specs,
        out_specs=pl.BlockSpec((tm, D_MODEL), lambda i: (i, 0)),
        out_shape=jax.ShapeDtypeStruct((t, D_MODEL), F32),
        compiler_params=_cparams("parallel"),
        name="outproj_residual",
    )(*acts, w, x, g, mod)


def _mlp_kernel(x_ref, gin_ref, gout_ref, mod_ref, w1_ref, w2_ref, o_ref, h_ref, acc_ref):
    f = pl.program_id(1)

    @pl.when(f == 0)
    def _():
        m = mod_ref[0]
        h_ref[...] = _modulate(x_ref[...], gin_ref[...], m[3:4], m[4:5]).astype(BF16)

    a = jnp.maximum(_dot(h_ref[...], w1_ref[...]), 0.0)
    part = _dot((a * a).astype(BF16), w2_ref[...])

    @pl.when(f == 0)
    def _():
        acc_ref[...] = part

    @pl.when(f > 0)
    def _():
        acc_ref[...] += part

    @pl.when(f == pl.num_programs(1) - 1)
    def _():
        m = mod_ref[0]
        o_ref[...] = x_ref[...] + m[5:6] * _rms(acc_ref[...], gout_ref[...])


def _mlp_residual(x, g_in, g_out, mod, w1, w2, tokens_per_row):
    t = x.shape[0]
    tm = min(1024, t)
    tf = 2048
    return pl.pallas_call(
        _mlp_kernel,
        grid=(t // tm, D_FF // tf),
        in_specs=[
            pl.BlockSpec((tm, D_MODEL), lambda i, f: (i, 0)),
            pl.BlockSpec((1, D_MODEL), lambda i, f: (0, 0)),
            pl.BlockSpec((1, D_MODEL), lambda i, f: (0, 0)),
            pl.BlockSpec((1, 6, D_MODEL), lambda i, f: (i // (tokens_per_row // tm), 0, 0)),
            pl.BlockSpec((D_MODEL, tf), lambda i, f: (0, f)),
            pl.BlockSpec((tf, D_MODEL), lambda i, f: (f, 0)),
        ],
        out_specs=pl.BlockSpec((tm, D_MODEL), lambda i, f: (i, 0)),
        out_shape=jax.ShapeDtypeStruct((t, D_MODEL), F32),
        scratch_shapes=[pltpu.VMEM((tm, D_MODEL), BF16), pltpu.VMEM((tm, D_MODEL), F32)],
        compiler_params=_cparams("parallel", "arbitrary"),
        name="mlp_residual",
    )(x, g_in, g_out, mod, w1, w2)


def _value_ones(n):
    lane = lax.broadcasted_iota(jnp.int32, (1, n), 1)
    return jnp.where((lane & HEAD_DIM) != 0, 1.0, 0.0)


def _rope_lanes(y, table, keep):
    lane = lax.broadcasted_iota(jnp.int32, y.shape, 1)
    prod = y * table
    rot = prod + pltpu.roll(prod, LANES - MLA_ROPE, 1)
    return jnp.where((lane >= MLA_NOPE) & (lane < MLA_NOPE + MLA_ROPE), rot, keep)


def _proj_mla_kernel(*refs, rope):
    (x_ref, g_ref, mod_ref, win_ref, qn_ref, kvn_ref, wuq_ref, wuk_ref, wuv_ref) = refs[:9]
    refs = refs[9:]
    if rope:
        tab_ref, refs = refs[0], refs[1:]
        qlat_ref, refs = refs[0], refs[1:]
    qctx_ref, k_ref, v_ref, ckv_ref, kpe_ref = refs

    m = mod_ref[0]
    h = _modulate(x_ref[...], g_ref[...], m[0:1], m[1:2]).astype(BF16)
    y = _dot(h, win_ref[...])
    c_q = y[:, :MLA_Q_LORA]
    c_kv = _rms(y[:, MLA_Q_LORA:MLA_Q_LORA + MLA_KV_LORA], kvn_ref[...])
    pe_blk = y[:, MLA_Q_LORA + MLA_KV_LORA:]
    ckv_ref[...] = c_kv
    kpe_ref[...] = pe_blk[:, MLA_NOPE:MLA_NOPE + MLA_ROPE]

    lane = lax.broadcasted_iota(jnp.int32, pe_blk.shape, 1)
    if rope:
        table = tab_ref[...]
        k_pe = _rope_lanes(pe_blk, table, jnp.zeros_like(pe_blk))
    else:
        k_pe = jnp.where((lane >= MLA_NOPE) & (lane < MLA_NOPE + MLA_ROPE), pe_blk, 0.0)

    q = _dot(_rms(c_q, qn_ref[...]).astype(BF16), wuq_ref[...])
    c_kv_b = c_kv.astype(BF16)
    kn = _dot(c_kv_b, wuk_ref[...])
    v_ref[...] = (_dot(c_kv_b, wuv_ref[...]) + _value_ones(v_ref.shape[-1])).astype(v_ref.dtype)
    for hd in range(MLA_HEADS):
        lanes = slice(hd * MLA_QK_PAD, (hd + 1) * MLA_QK_PAD)
        qh = q[:, lanes]
        qctx_ref[:, lanes] = qh.astype(qctx_ref.dtype)
        if rope:
            qlat_ref[:, lanes] = _rope_lanes(qh, table, qh).astype(qlat_ref.dtype)
        k_ref[:, lanes] = (kn[:, lanes] + k_pe).astype(k_ref.dtype)


def _proj_mla(x, g, mod, w_in, q_norm, kv_norm, w_uq, w_uk, w_uv, tokens_per_row, rope_table):
    t = x.shape[0]
    tm = min(512, t)
    rope = rope_table is not None
    full = lambda a: pl.BlockSpec(a.shape, lambda i: (0,) * a.ndim)
    tok = lambda n: pl.BlockSpec((tm, n), lambda i: (i, 0))
    in_specs = [tok(D_MODEL), full(g), _mod_spec(tokens_per_row // tm), full(w_in), full(q_norm),
                full(kv_norm), full(w_uq), full(w_uk), full(w_uv)]
    args = [x, g, mod, w_in, q_norm, kv_norm, w_uq, w_uk, w_uv]
    qk = MLA_HEADS * MLA_QK_PAD
    out_specs, out_shape = [], []
    if rope:
        tiles_per_seq = rope_table.shape[0] // tm
        in_specs.append(pl.BlockSpec((tm, LANES), lambda i: (i % tiles_per_seq, 0)))
        args.append(rope_table)
        out_specs.append(tok(qk))
        out_shape.append(jax.ShapeDtypeStruct((t, qk), BF16))
    out_specs += [tok(qk), tok(qk), tok(MLA_HEADS * HEAD_PAIR), tok(MLA_KV_LORA), tok(MLA_ROPE)]
    out_shape += [jax.ShapeDtypeStruct((t, qk), BF16), jax.ShapeDtypeStruct((t, qk), BF16),
                  jax.ShapeDtypeStruct((t, MLA_HEADS * HEAD_PAIR), BF16),
                  jax.ShapeDtypeStruct((t, MLA_KV_LORA), F32), jax.ShapeDtypeStruct((t, MLA_ROPE), F32)]
    return pl.pallas_call(
        functools.partial(_proj_mla_kernel, rope=rope),
        grid=(t // tm,),
        in_specs=in_specs,
        out_specs=out_specs,
        out_shape=out_shape,
        compiler_params=_cparams("parallel"),
        name="proj_mla",
    )(*args)


def _ctx_kv_kernel(ckv_ref, kpe_ref, wuk_ref, wuv_ref, place_ref, k_ref, v_ref):
    c = ckv_ref[...].astype(BF16)
    kn = _dot(c, wuk_ref[...])
    k_pe = _dot(kpe_ref[...].astype(BF16), place_ref[...])
    v_ref[...] = (_dot(c, wuv_ref[...]) + _value_ones(v_ref.shape[-1])).astype(v_ref.dtype)
    for hd in range(MLA_HEADS):
        lanes = slice(hd * MLA_QK_PAD, (hd + 1) * MLA_QK_PAD)
        k_ref[:, lanes] = (kn[:, lanes] + k_pe).astype(k_ref.dtype)


def _ctx_kv(ckv, kpe, w_uk, w_uv, place):
    t = ckv.shape[0]
    tm = min(512, t)
    full = lambda a: pl.BlockSpec(a.shape, lambda i: (0,) * a.ndim)
    tok = lambda n: pl.BlockSpec((tm, n), lambda i: (i, 0))
    return pl.pallas_call(
        _ctx_kv_kernel,
        grid=(t // tm,),
        in_specs=[tok(MLA_KV_LORA), tok(MLA_ROPE), full(w_uk), full(w_uv), full(place)],
        out_specs=[tok(MLA_HEADS * MLA_QK_PAD), tok(MLA_HEADS * HEAD_PAIR)],
        out_shape=[jax.ShapeDtypeStruct((t, MLA_HEADS * MLA_QK_PAD), BF16),
                   jax.ShapeDtypeStruct((t, MLA_HEADS * HEAD_PAIR), BF16)],
        compiler_params=_cparams("parallel"),
        name="mla_ctx_kv",
    )(ckv, kpe, w_uk, w_uv, place)


def _pair_swap(w):
    return w.reshape(w.shape[:-1] + (w.shape[-1] // 2, 2))[..., ::-1].reshape(w.shape)


def _mla_weights(w_in, w_uq, w_uk, w_uv):
    k_pe_cols = w_in[:, MLA_Q_LORA + MLA_KV_LORA:]
    w_in_ext = jnp.concatenate(
        [w_in[:, :MLA_Q_LORA + MLA_KV_LORA], jnp.zeros((D_MODEL, MLA_NOPE), w_in.dtype),
         k_pe_cols, _pair_swap(k_pe_cols)], axis=1).astype(BF16)
    uq = w_uq.reshape(MLA_Q_LORA, MLA_HEADS, MLA_NOPE + MLA_ROPE)
    uq_ext = jnp.concatenate([uq, _pair_swap(uq[..., MLA_NOPE:])], axis=-1)
    uq_ext = uq_ext.reshape(MLA_Q_LORA, MLA_HEADS * MLA_QK_PAD).astype(BF16)
    uk = w_uk.reshape(MLA_KV_LORA, MLA_HEADS, MLA_NOPE)
    uk_ext = jnp.concatenate([uk, jnp.zeros((MLA_KV_LORA, MLA_HEADS, MLA_QK_PAD - MLA_NOPE), uk.dtype)], axis=-1)
    uk_ext = uk_ext.reshape(MLA_KV_LORA, MLA_HEADS * MLA_QK_PAD).astype(BF16)
    uv = w_uv.reshape(MLA_KV_LORA, MLA_HEADS, MLA_V)
    uv_ext = jnp.concatenate([uv, jnp.zeros((MLA_KV_LORA, MLA_HEADS, HEAD_PAIR - MLA_V), uv.dtype)], axis=-1)
    uv_ext = uv_ext.reshape(MLA_KV_LORA, MLA_HEADS * HEAD_PAIR).astype(BF16)
    return w_in_ext, uq_ext, uk_ext, uv_ext


def _rope_table(l):
    t = jnp.arange(l)
    row = (t // GRID_W).astype(F32)
    col = (t % GRID_W).astype(F32)
    nf = MLA_ROPE // 4
    inv = ROPE_BASE ** (-jnp.arange(nf, dtype=F32) / nf)
    ang = jnp.concatenate([row[:, None] * inv, col[:, None] * inv], axis=-1)
    cos, sin = jnp.cos(ang), jnp.sin(ang)
    cc = jnp.repeat(cos, 2, axis=-1)
    ss = jnp.stack([-sin, sin], axis=-1).reshape(l, MLA_ROPE)
    return jnp.concatenate([jnp.ones((l, MLA_NOPE), F32), cc, ss], axis=-1)


def kernel(x_prompt, x_sample, state_ret_fwd, state_ret_bwd, cache_na_k, cache_na_v, cache_mla_ckv,
           cache_mla_kpe, c, c_ctx, w_ada, b_ada, norm_gains, w_mlp_in, w_mlp_out, w_in_ac, w_out_ac,
           ret_decay_fwd, ret_decay_bwd, na_rpb, w_in_c, mla_q_norm, mla_kv_norm, w_uq, w_uk, w_uv, w_out_c):
    bp, lp, _ = x_prompt.shape
    bs, ls, _ = x_sample.shape
    lc = cache_mla_ckv.shape[2]
    xp = x_prompt.reshape(bp * lp, D_MODEL)
    xs = x_sample.reshape(bs * ls, D_MODEL)

    n_cond = bs + 1
    cond = jnp.concatenate([c, c_ctx[None]], axis=0)
    mods = _modulation_all(cond, w_ada, b_ada)
    rope_table = _rope_table(ls)
    place = jnp.zeros((MLA_ROPE, MLA_QK_PAD), BF16).at[
        jnp.arange(MLA_ROPE), MLA_NOPE + jnp.arange(MLA_ROPE)].set(1.0)

    ret_f, ret_b, na_k, na_v, mla_ckv, mla_kpe = [], [], [], [], [], []
    for layer in range(DEPTH):
        mod_s = mods[layer, :bs]
        mod_p = mods[layer, bs:n_cond]
        g = norm_gains[layer]
        g0, g1, g2, g3 = (g[i:i + 1] for i in range(4))
        tp, ts = bp * lp, ls
        if layer % 2 == 0:
            ia = layer // 2
            w_in = w_in_ac[ia].astype(BF16)
            w_out = w_out_ac[ia].astype(BF16)
            lg_f = jnp.log1p(-jnp.exp2(ret_decay_fwd[ia].astype(F32)))
            lg_b = jnp.log1p(-jnp.exp2(ret_decay_bwd[ia].astype(F32)))
            seq = lambda a, b, l: a.reshape(b, l, a.shape[-1])

            rq, rk, rv, rg, nq, nk, nv = [seq(a, bp, lp) for a in _proj_even(xp, g0, mod_p, w_in, tp, F32)]
            o_ret, s_f, s_b = _retention(rq, rk, rv, rg, lg_f, lg_b, None, None, ia, True)
            o_na, k_out, v_out = _attention([nq], [nk], [nv], H_NA, HEAD_DIM, HEAD_DIM ** -0.5, lp,
                                             heads_per_step=H_NA, emit_kv=True)
            ret_f.append(s_f)
            ret_b.append(s_b)
            na_k.append(k_out)
            na_v.append(v_out)
            xp = _outproj_residual([o_ret.reshape(tp, -1), o_na.reshape(tp, -1)], w_out, xp, g1, mod_p, tp, 2)

            rq, rk, rv, rg, nq, nk, nv = [seq(a, bs, ls) for a in _proj_even(xs, g0, mod_s, w_in, ts, BF16)]
            o_ret = _retention(rq, rk, rv, rg, lg_f, lg_b, state_ret_fwd, state_ret_bwd, ia, False)[0]
            o_na = _na_latent(nq, nk, nv, cache_na_k, cache_na_v, ia, _na_bias_tables(na_rpb[ia], ls // GRID_W))
            xs = _outproj_residual([o_ret.reshape(bs * ls, -1), o_na.reshape(bs * ls, -1)], w_out, xs, g1,
                                   mod_s, ts, 2)
        else:
            ic = layer // 2
            w_in_ext, uq_ext, uk_ext, uv = _mla_weights(w_in_c[ic], w_uq[ic], w_uk[ic], w_uv[ic])
            w_out = w_out_c[ic].astype(BF16)
            qn, kvn = mla_q_norm[ic][None], mla_kv_norm[ic][None]

            q, k, v, ckv, kpe = _proj_mla(xp, g0, mod_p, w_in_ext, qn, kvn, uq_ext, uk_ext, uv, tp, None)
            mla_ckv.append(ckv.reshape(bp, lp, MLA_KV_LORA))
            mla_kpe.append(kpe.reshape(bp, lp, MLA_ROPE))
            sq = lambda a, b, l: a.reshape(b, l, a.shape[-1])
            o = _attention([sq(q, bp, lp)], [sq(k, bp, lp)], [sq(v, bp, lp)], MLA_HEADS, MLA_QK_PAD,
                           MLA_SCALE, lp, heads_per_step=8, v_stride=HEAD_PAIR)
            xp = _outproj_residual([o.reshape(tp, -1)], w_out, xp, g1, mod_p, tp, 2)

            q_lat, q_ctx, k, v, _, _ = _proj_mla(xs, g0, mod_s, w_in_ext, qn, kvn, uq_ext, uk_ext, uv, ts,
                                                 rope_table)
            k_c, v_c = _ctx_kv(cache_mla_ckv[:, ic].reshape(bs * lc, MLA_KV_LORA),
                               cache_mla_kpe[:, ic].reshape(bs * lc, MLA_ROPE), uk_ext, uv, place)
            o = _attention_aug([sq(q_lat, bs, ls), sq(q_ctx, bs, ls)], [sq(k, bs, ls), sq(k_c, bs, lc)],
                               [sq(v, bs, ls), sq(v_c, bs, lc)], MLA_HEADS, MLA_QK_PAD, MLA_SCALE, 1024,
                               heads_per_step=8)
            xs = _outproj_residual([o.reshape(bs * ls, -1)], w_out, xs, g1, mod_s, ts, 2)

        w1 = w_mlp_in[layer].astype(BF16)
        w2 = w_mlp_out[layer].astype(BF16)
        xp = _mlp_residual(xp, g2, g3, mod_p, w1, w2, tp)
        xs = _mlp_residual(xs, g2, g3, mod_s, w1, w2, ts)

    return (xp.reshape(bp, lp, D_MODEL), xs.reshape(bs, ls, D_MODEL),
            jnp.stack(ret_f, axis=1), jnp.stack(ret_b, axis=1),
            jnp.stack(na_k, axis=1), jnp.stack(na_v, axis=1),
            jnp.stack(mla_ckv, axis=1), jnp.stack(mla_kpe, axis=1))
```
